```python
import math
import jax, jax.numpy as jnp
from jax import lax
import numpy as np

D_MODEL = 2048
BATCH = 2
SEQ = 8192
DEPTH = 1

MEM_LEN = 256
D_FF = 256 * math.ceil(8 * D_MODEL / 3 / 256)
BRANCH_WIDTH = D_MODEL // 2
N_BRANCH = 3
CONV_WIDTH = BRANCH_WIDTH
CONV_K = 3
NSA_HEAD_DIM = 64
NSA_HEADS = BRANCH_WIDTH // NSA_HEAD_DIM
NSA_GROUPS = 4
NSA_HPG = NSA_HEADS // NSA_GROUPS
NSA_KV = NSA_GROUPS * NSA_HEAD_DIM
CMP_LEN = 32
CMP_STRIDE = 16
CMP_HIDDEN = 4 * NSA_HEAD_DIM
SLC_LEN = 64
SLC_TOP = 16
WIN = 512
Q_BLOCK = 128
MEM_HEADS = 4
MEM_HEAD_DIM = BRANCH_WIDTH // MEM_HEADS
EPS = 1e-6
NEG_INF = -1e30
FORCE_SCORE = 1e9
IN_SPLITS = (CONV_WIDTH, CONV_WIDTH, CONV_WIDTH,
             NSA_HEADS * NSA_HEAD_DIM,
             6 * NSA_KV,
             3 * NSA_HEADS,
             MEM_HEADS * MEM_HEAD_DIM,
             N_BRANCH * D_MODEL)
D_IN = sum(IN_SPLITS)

kernel_name = 'hybrid_conv_nsa_mem_macaron'


def rms_norm(x, g):
    xf = x.astype(jnp.float32)
    y = xf * lax.rsqrt(jnp.mean(xf * xf, axis=-1, keepdims=True) + EPS)
    return (y * g.astype(jnp.float32)).astype(x.dtype)


def swiglu(h, w_gate, w_up, w_down):
    return (jax.nn.silu(h @ w_gate) * (h @ w_up)) @ w_down


def masked_softmax(s, valid):
    s = jnp.where(valid, s.astype(jnp.float32), NEG_INF)
    p = jax.nn.softmax(s, axis=-1)
    return jnp.where(valid, p, 0.0)


def short_conv_mixer(b, c, u, conv_w):
    z = c * u
    y = lax.conv_general_dilated(z, conv_w[:, None, :], window_strides=(1,),
                                 padding=[(CONV_K - 1, 0)],
                                 dimension_numbers=('NWC', 'WIO', 'NWC'),
                                 feature_group_count=CONV_WIDTH)
    return b * y


def compress_blocks(t, pe, w1, w2):
    B, S = t.shape[:2]
    n_c = (S - CMP_LEN) // CMP_STRIDE + 1
    idx = jnp.arange(n_c)[:, None] * CMP_STRIDE + jnp.arange(CMP_LEN)[None, :]
    blk = t[:, idx] + pe[None, None, :, None, :]
    blk = jnp.moveaxis(blk, 3, 2).reshape(B, n_c, NSA_GROUPS, CMP_LEN * NSA_HEAD_DIM)
    return jax.nn.silu(blk @ w1) @ w2


def selection_map(n_c, n_s):
    ratio = SLC_LEN // CMP_STRIDE
    offs = np.arange(-(CMP_LEN // CMP_STRIDE - 1), ratio)
    ci = ratio * np.arange(n_s)[:, None] + offs[None, :]
    cs = ci * CMP_STRIDE
    ss = np.arange(n_s)[:, None] * SLC_LEN
    ov = np.clip(np.minimum(cs + CMP_LEN, ss + SLC_LEN) - np.maximum(cs, ss), 0, None)
    ok = (ci >= 0) & (ci < n_c)
    w = np.where(ok, ov / CMP_LEN, 0.0).astype(np.float32)
    return jnp.asarray(np.clip(ci, 0, n_c - 1)), jnp.asarray(w)


def nsa_attention(q, kv, gate_logits, g_q, g_kc, g_ks, g_kw,
                  pe_k, w1_k, w2_k, pe_v, w1_v, w2_v):
    B, S = q.shape[:2]
    G, HPG, DH = NSA_GROUPS, NSA_HPG, NSA_HEAD_DIM
    scale = DH ** -0.5
    q = rms_norm(q.reshape(B, S, NSA_HEADS, DH), g_q).reshape(B, S, G, HPG, DH)
    kc, vc, ks, vs, kw, vw = [t.reshape(B, S, G, DH) for t in jnp.split(kv, 6, axis=-1)]
    k_c = rms_norm(compress_blocks(kc, pe_k, w1_k, w2_k), g_kc)
    v_c = compress_blocks(vc, pe_v, w1_v, w2_v)
    n_c = k_c.shape[1]
    n_s = S // SLC_LEN
    n_top = min(SLC_TOP, n_s)
    k_s = rms_norm(ks, g_ks).reshape(B, n_s, SLC_LEN, G, DH).transpose(0, 3, 1, 2, 4)
    v_s = vs.reshape(B, n_s, SLC_LEN, G, DH).transpose(0, 3, 1, 2, 4)
    pad = ((0, 0), (WIN, 0), (0, 0), (0, 0))
    k_w = jnp.pad(rms_norm(kw, g_kw), pad)
    v_w = jnp.pad(vw, pad)
    gates = jax.nn.sigmoid(gate_logits.reshape(B, S, G, HPG, 3))
    slc_idx, slc_w = selection_map(n_c, n_s)
    cmp_end = jnp.arange(n_c) * CMP_STRIDE + CMP_LEN - 1
    blk_ids = jnp.arange(n_s)
    b_ix = jnp.arange(B)[:, None, None, None]
    g_ix = jnp.arange(G)[None, None, :, None]

    def block(c):
        t0 = c * Q_BLOCK
        t = t0 + jnp.arange(Q_BLOCK)
        qb = lax.dynamic_slice_in_dim(q, t0, Q_BLOCK, axis=1)
        s_c = jnp.einsum('bqghd,bngd->bqghn', qb, k_c) * scale
        valid_c = cmp_end[None, :] <= t[:, None]
        p_c = masked_softmax(s_c, valid_c[None, :, None, None, :])
        o_c = jnp.einsum('bqghn,bngd->bqghd', p_c.astype(v_c.dtype), v_c)
        imp_c = jnp.sum(p_c, axis=3)
        imp_s = jnp.sum(imp_c[..., slc_idx] * slc_w, axis=-1)
        cur = t // SLC_LEN
        valid_s = blk_ids[None, :] <= cur[:, None]
        forced = ((blk_ids[None, :] == 0) | (blk_ids[None, :] == cur[:, None])
                  | (blk_ids[None, :] == cur[:, None] - 1))
        score = jnp.where(forced[None, :, None, :], FORCE_SCORE,
                          jnp.where(valid_s[None, :, None, :], imp_s, NEG_INF))
        _, sel = lax.top_k(score, n_top)
        k_sel = k_s[b_ix, g_ix, sel].reshape(B, Q_BLOCK, G, n_top * SLC_LEN, DH)
        v_sel = v_s[b_ix, g_ix, sel].reshape(B, Q_BLOCK, G, n_top * SLC_LEN, DH)
        kpos = (sel[..., None] * SLC_LEN + jnp.arange(SLC_LEN)).reshape(B, Q_BLOCK, G, n_top * SLC_LEN)
        valid_sel = kpos <= t[None, :, None, None]
        s_s = jnp.einsum('bqghd,bqgkd->bqghk', qb, k_sel) * scale
        p_s = masked_softmax(s_s, valid_sel[:, :, :, None, :])
        o_s = jnp.einsum('bqghk,bqgkd->bqghd', p_s.astype(v_sel.dtype), v_sel)
        kw_b = lax.dynamic_slice_in_dim(k_w, t0, Q_BLOCK + WIN, axis=1)
        vw_b = lax.dynamic_slice_in_dim(v_w, t0, Q_BLOCK + WIN, axis=1)
        pos = t0 - WIN + jnp.arange(Q_BLOCK + WIN)
        valid_w = (pos[None, :] >= 0) & (pos[None, :] <= t[:, None]) & (pos[None, :] > t[:, None] - WIN)
        s_w = jnp.einsum('bqghd,bkgd->bqghk', qb, kw_b) * scale
        p_w = masked_softmax(s_w, valid_w[None, :, None, None, :])
        o_w = jnp.einsum('bqghk,bkgd->bqghd', p_w.astype(vw_b.dtype), vw_b)
        gb = lax.dynamic_slice_in_dim(gates, t0, Q_BLOCK, axis=1)
        o = gb[..., 0:1] * o_c + gb[..., 1:2] * o_s + gb[..., 2:3] * o_w
        return o.reshape(B, Q_BLOCK, NSA_HEADS * DH)

    out = lax.map(block, jnp.arange(S // Q_BLOCK))
    return jnp.moveaxis(out, 0, 1).reshape(B, S, NSA_HEADS * DH)


def memory_cross_attention(q, mem_h, w_kv, g_q, g_k):
    B, S = q.shape[:2]
    M = mem_h.shape[1]
    km, vm = jnp.split(mem_h @ w_kv, 2, axis=-1)
    km = rms_norm(km.reshape(B, M, MEM_HEADS, MEM_HEAD_DIM), g_k)
    vm = vm.reshape(B, M, MEM_HEADS, MEM_HEAD_DIM)
    q = rms_norm(q.reshape(B, S, MEM_HEADS, MEM_HEAD_DIM), g_q)
    s = jnp.einsum('bshd,bmhd->bhsm', q, km).astype(jnp.float32) * MEM_HEAD_DIM ** -0.5
    p = jax.nn.softmax(s, axis=-1).astype(vm.dtype)
    return jnp.einsum('bhsm,bmhd->bshd', p, vm).reshape(B, S, BRANCH_WIDTH)


def setup_inputs(seed: int = 0) -> dict:
    key = jax.random.key(seed)
    ks = iter(jax.random.split(key, 40))

    def w(shape, fan_in):
        return jax.random.normal(next(ks), (DEPTH,) + shape, jnp.float32) * fan_in ** -0.5

    def gain(n):
        return 1.0 + 0.02 * jax.random.normal(next(ks), (DEPTH, n), jnp.float32)

    x = jax.random.normal(next(ks), (BATCH, SEQ, D_MODEL), jnp.float32)
    mem = jax.random.normal(next(ks), (BATCH, MEM_LEN, D_MODEL), jnp.float32)
    return {
        'x': x,
        'mem': mem,
        'ffn1_norm': gain(D_MODEL),
        'ffn1_w_gate': w((D_MODEL, D_FF), D_MODEL),
        'ffn1_w_up': w((D_MODEL, D_FF), D_MODEL),
        'ffn1_w_down': w((D_FF, D_MODEL), D_FF),
        'mix_norm': gain(D_MODEL),
        'mem_norm': gain(D_MODEL),
        'w_in': w((D_MODEL, D_IN), D_MODEL),
        'conv_w': w((CONV_K, CONV_WIDTH), CONV_K),
        'nsa_q_norm': gain(NSA_HEAD_DIM),
        'nsa_kc_norm': gain(NSA_HEAD_DIM),
        'nsa_ks_norm': gain(NSA_HEAD_DIM),
        'nsa_kw_norm': gain(NSA_HEAD_DIM),
        'cmp_pe_k': 0.1 * jax.random.normal(next(ks), (DEPTH, CMP_LEN, NSA_HEAD_DIM), jnp.float32),
        'cmp_w1_k': w((CMP_LEN * NSA_HEAD_DIM, CMP_HIDDEN), CMP_LEN * NSA_HEAD_DIM),
        'cmp_w2_k': w((CMP_HIDDEN, NSA_HEAD_DIM), CMP_HIDDEN),
        'cmp_pe_v': 0.1 * jax.random.normal(next(ks), (DEPTH, CMP_LEN, NSA_HEAD_DIM), jnp.float32),
        'cmp_w1_v': w((CMP_LEN * NSA_HEAD_DIM, CMP_HIDDEN), CMP_LEN * NSA_HEAD_DIM),
        'cmp_w2_v': w((CMP_HIDDEN, NSA_HEAD_DIM), CMP_HIDDEN),
        'w_mem_kv': w((D_MODEL, 2 * BRANCH_WIDTH), D_MODEL),
        'mem_q_norm': gain(MEM_HEAD_DIM),
        'mem_k_norm': gain(MEM_HEAD_DIM),
        'w_branch': w((N_BRANCH, BRANCH_WIDTH, D_MODEL), BRANCH_WIDTH),
        'w_o': w((D_MODEL, D_MODEL), D_MODEL),
        'ffn2_norm': gain(D_MODEL),
        'ffn2_w_gate': w((D_MODEL, D_FF), D_MODEL),
        'ffn2_w_up': w((D_MODEL, D_FF), D_MODEL),
        'ffn2_w_down': w((D_FF, D_MODEL), D_FF),
    }


def reference(x, mem, ffn1_norm, ffn1_w_gate, ffn1_w_up, ffn1_w_down, mix_norm, mem_norm,
              w_in, conv_w, nsa_q_norm, nsa_kc_norm, nsa_ks_norm, nsa_kw_norm,
              cmp_pe_k, cmp_w1_k, cmp_w2_k, cmp_pe_v, cmp_w1_v, cmp_w2_v,
              w_mem_kv, mem_q_norm, mem_k_norm, w_branch, w_o,
              ffn2_norm, ffn2_w_gate, ffn2_w_up, ffn2_w_down):
    B, S, D = x.shape
    split_at = np.cumsum(IN_SPLITS)[:-1].tolist()
    for l in range(DEPTH):
        h = rms_norm(x, ffn1_norm[l])
        x = x + 0.5 * swiglu(h, ffn1_w_gate[l], ffn1_w_up[l], ffn1_w_down[l])
        h = rms_norm(x, mix_norm[l])
        z = h @ w_in[l]
        b_g, c_g, u, q_nsa, kv_nsa, g_nsa, q_mem, g_merge = jnp.split(z, split_at, axis=-1)
        y_conv = short_conv_mixer(b_g, c_g, u, conv_w[l])
        y_nsa = nsa_attention(q_nsa, kv_nsa, g_nsa, nsa_q_norm[l], nsa_kc_norm[l],
                              nsa_ks_norm[l], nsa_kw_norm[l], cmp_pe_k[l], cmp_w1_k[l],
                              cmp_w2_k[l], cmp_pe_v[l], cmp_w1_v[l], cmp_w2_v[l])
        y_mem = memory_cross_attention(q_mem, rms_norm(mem, mem_norm[l]), w_mem_kv[l],
                                       mem_q_norm[l], mem_k_norm[l])
        ys = jnp.stack([y_conv, y_nsa, y_mem], axis=2)
        yb = jnp.einsum('bsnc,ncd->bsnd', ys, w_branch[l])
        gate = jax.nn.sigmoid(g_merge.reshape(B, S, N_BRANCH, D))
        merged = jnp.einsum('bsnd,bsnd->bsd', gate, yb)
        x = x + merged @ w_o[l]
        h = rms_norm(x, ffn2_norm[l])
        x = x + 0.5 * swiglu(h, ffn2_w_gate[l], ffn2_w_up[l], ffn2_w_down[l])
    return x
```

```python
import functools
import math

import jax
import jax.numpy as jnp
import numpy as np
from jax import lax
from jax.experimental import pallas as pl
from jax.experimental.pallas import tpu as pltpu

F32 = jnp.float32
BF16 = jnp.bfloat16

EPS = 1e-6
NEG_INF = -1e30
FORCE_SCORE = 1e9

CONV_K = 3
NSA_HEAD_DIM = 64
NSA_GROUPS = 4
NSA_HPG = 4
NSA_HEADS = NSA_GROUPS * NSA_HPG
CMP_LEN = 32
CMP_STRIDE = 16
CMP_HIDDEN = 4 * NSA_HEAD_DIM
SLC_LEN = 64
SLC_TOP = 16
WIN = 512
Q_BLOCK = 128
MEM_HEADS = 4
N_BRANCH = 3

LANES = 128
SUBLANES = 8
VMEM_LIMIT_BYTES = 56 * 1024 * 1024

FFN_TM = 512
FFN_TF = 512
PROJ_TM = 512
PROJ_TN = 768
CONV_TS = 512
MEM_TM = 512
PACK_TS = 1024
SEL_TK = 512
MERGE_TM = 512
MERGE_TN = 512

Z_CONV = 0
Z_QNSA = 3072
Z_QMEM = 4096
Z_KV = 5120
Z_GATE = 6656
Z_COLS = 6912


def _params(sem):
    return pltpu.CompilerParams(dimension_semantics=sem, vmem_limit_bytes=VMEM_LIMIT_BYTES)


def _rms(x, g):
    return x * lax.rsqrt(jnp.mean(x * x, axis=-1, keepdims=True) + EPS) * g


def _sigmoid(x):
    return 1.0 / (1.0 + jnp.exp(-x))


def _dot(a, b):
    return jnp.dot(a, b, preferred_element_type=F32)


def _dot_nt(a, b):
    return lax.dot_general(a, b, (((1,), (1,)), ((), ())), preferred_element_type=F32)


def _half_norm(x, g):
    lane = lax.broadcasted_iota(jnp.int32, x.shape, 1)
    lo = lane < NSA_HEAD_DIM
    xx = x * x
    s_lo = jnp.sum(jnp.where(lo, xx, 0.0), axis=-1, keepdims=True)
    s_hi = jnp.sum(jnp.where(lo, 0.0, xx), axis=-1, keepdims=True)
    r = jnp.where(lo, lax.rsqrt(s_lo / NSA_HEAD_DIM + EPS), lax.rsqrt(s_hi / NSA_HEAD_DIM + EPS))
    return x * r * g


def _split_halves(x):
    lane = lax.broadcasted_iota(jnp.int32, x.shape, 1)
    lo = lane < NSA_HEAD_DIM
    return jnp.where(lo, x, 0.0), jnp.where(lo, pltpu.roll(x, NSA_HEAD_DIM, axis=1), 0.0)


def _ffn_kernel(x_ref, g_ref, wg_ref, wu_ref, wd_ref, o_ref, h_ref):
    j = pl.program_id(1)

    @pl.when(j == 0)
    def _():
        h_ref[...] = _rms(x_ref[...], g_ref[...]).astype(BF16)
        o_ref[...] = jnp.zeros_like(o_ref)

    h = h_ref[...]
    gate = _dot(h, wg_ref[...])
    up = _dot(h, wu_ref[...])
    act = (gate * _sigmoid(gate) * up).astype(BF16)
    o_ref[...] += _dot(act, wd_ref[...])

    @pl.when(j == pl.num_programs(1) - 1)
    def _():
        o_ref[...] = x_ref[...] + 0.5 * o_ref[...]


def _ffn(x, g, wg, wu, wd):
    n, d = x.shape
    f = wg.shape[1]
    return pl.pallas_call(
        _ffn_kernel,
        out_shape=jax.ShapeDtypeStruct((n, d), F32),
        grid=(n // FFN_TM, f // FFN_TF),
        in_specs=[
            pl.BlockSpec((FFN_TM, d), lambda i, j: (i, 0)),
            pl.BlockSpec((1, d), lambda i, j: (0, 0)),
            pl.BlockSpec((d, FFN_TF), lambda i, j: (0, j)),
            pl.BlockSpec((d, FFN_TF), lambda i, j: (0, j)),
            pl.BlockSpec((FFN_TF, d), lambda i, j: (j, 0)),
        ],
        out_specs=pl.BlockSpec((FFN_TM, d), lambda i, j: (i, 0)),
        scratch_shapes=[pltpu.VMEM((FFN_TM, d), BF16)],
        compiler_params=_params(("parallel", "arbitrary")),
        name="ffn",
    )(x, g, wg, wu, wd)


def _proj_kernel(x_ref, g_ref, w_ref, o_ref, h_ref):
    @pl.when(pl.program_id(1) == 0)
    def _():
        h_ref[...] = _rms(x_ref[...], g_ref[...]).astype(BF16)

    o_ref[...] = _dot(h_ref[...], w_ref[...])


def _in_proj(x, g, w):
    n, d = x.shape
    c = w.shape[1]
    return pl.pallas_call(
        _proj_kernel,
        out_shape=jax.ShapeDtypeStruct((n, c), F32),
        grid=(n // PROJ_TM, c // PROJ_TN),
        in_specs=[
            pl.BlockSpec((PROJ_TM, d), lambda i, j: (i, 0)),
            pl.BlockSpec((1, d), lambda i, j: (0, 0)),
            pl.BlockSpec((d, PROJ_TN), lambda i, j: (0, j)),
        ],
        out_specs=pl.BlockSpec((PROJ_TM, PROJ_TN), lambda i, j: (i, j)),
        scratch_shapes=[pltpu.VMEM((PROJ_TM, d), BF16)],
        compiler_params=_params(("parallel", "arbitrary")),
        name="in_proj",
    )(x, g, w)


def _conv_kernel(b_ref, c_ref, u_ref, cp_ref, up_ref, w_ref, o_ref, z_ref):
    ts = b_ref.shape[0]
    first = pl.program_id(1) == 0
    zp = cp_ref[...] * up_ref[...]
    z_ref[pl.ds(0, SUBLANES), :] = jnp.where(first, 0.0, zp)
    z = c_ref[...] * u_ref[...]
    z_ref[pl.ds(SUBLANES, ts), :] = z
    z1 = z_ref[pl.ds(SUBLANES - 1, ts), :]
    z2 = z_ref[pl.ds(SUBLANES - 2, ts), :]
    w = w_ref[...]
    y = b_ref[...] * (w[0:1, :] * z2 + w[1:2, :] * z1 + w[2:3, :] * z)
    o_ref[...] = y.astype(o_ref.dtype)


def _conv(z, conv_w, batch, seq):
    cw = conv_w.shape[1]
    nst = seq // CONV_TS
    rows_per = CONV_TS // SUBLANES
    cb = Z_CONV // cw

    def prev_map(col):
        return lambda b, i: (jnp.maximum((b * nst + i) * rows_per - 1, 0), col)

    return pl.pallas_call(
        _conv_kernel,
        out_shape=jax.ShapeDtypeStruct((batch * seq, cw), BF16),
        grid=(batch, nst),
        in_specs=[
            pl.BlockSpec((CONV_TS, cw), lambda b, i: (b * nst + i, cb)),
            pl.BlockSpec((CONV_TS, cw), lambda b, i: (b * nst + i, cb + 1)),
            pl.BlockSpec((CONV_TS, cw), lambda b, i: (b * nst + i, cb + 2)),
            pl.BlockSpec((SUBLANES, cw), prev_map(cb + 1)),
            pl.BlockSpec((SUBLANES, cw), prev_map(cb + 2)),
            pl.BlockSpec((CONV_K, cw), lambda b, i: (0, 0)),
        ],
        out_specs=pl.BlockSpec((CONV_TS, cw), lambda b, i: (b * nst + i, 0)),
        scratch_shapes=[pltpu.VMEM((CONV_TS + SUBLANES, cw), F32)],
        compiler_params=_params(("parallel", "arbitrary")),
        name="conv",
    )(z, z, z, z, z, conv_w)


def _mem_kv_kernel(m_ref, g_ref, w_ref, gk_ref, o_ref, *, n_key_tiles):
    h = _rms(m_ref[...], g_ref[...]).astype(BF16)
    kv = _dot(h, w_ref[...])
    is_key = pl.program_id(0) < n_key_tiles
    o_ref[...] = jnp.where(is_key, _rms(kv, gk_ref[...]), kv).astype(o_ref.dtype)


def _mem_kv(mem2d, g, w, gk):
    rows, d = mem2d.shape
    hd = gk.shape[1]
    cols = w.shape[1]
    return pl.pallas_call(
        functools.partial(_mem_kv_kernel, n_key_tiles=cols // (2 * hd)),
        out_shape=jax.ShapeDtypeStruct((rows, cols), BF16),
        grid=(cols // hd,),
        in_specs=[
            pl.BlockSpec((rows, d), lambda j: (0, 0)),
            pl.BlockSpec((1, d), lambda j: (0, 0)),
            pl.BlockSpec((d, hd), lambda j: (0, j)),
            pl.BlockSpec((1, hd), lambda j: (0, 0)),
        ],
        out_specs=pl.BlockSpec((rows, hd), lambda j: (0, j)),
        compiler_params=_params(("arbitrary",)),
        name="mem_kv",
    )(mem2d, g, w, gk)


def _mem_attn_kernel(q_ref, k_ref, v_ref, gq_ref, o_ref):
    hd = gq_ref.shape[1]
    scale = hd ** -0.5
    outs = []
    for h in range(MEM_HEADS):
        sl = slice(h * hd, (h + 1) * hd)
        q = (_rms(q_ref[:, sl], gq_ref[...]) * scale).astype(BF16)
        s = _dot_nt(q, k_ref[:, sl])
        e = jnp.exp(s - jnp.max(s, axis=-1, keepdims=True))
        p = e / jnp.sum(e, axis=-1, keepdims=True)
        outs.append(_dot(p.astype(BF16), v_ref[:, sl]))
    o_ref[...] = jnp.concatenate(outs, axis=1).astype(o_ref.dtype)


def _mem_attn(z, kvm, gq, batch, seq, mem_len):
    width = MEM_HEADS * gq.shape[1]
    nt = seq // MEM_TM
    return pl.pallas_call(
        _mem_attn_kernel,
        out_shape=jax.ShapeDtypeStruct((batch * seq, width), BF16),
        grid=(batch, nt),
        in_specs=[
            pl.BlockSpec((MEM_TM, width), lambda b, i: (b * nt + i, Z_QMEM // width)),
            pl.BlockSpec((mem_len, width), lambda b, i: (b, 0)),
            pl.BlockSpec((mem_len, width), lambda b, i: (b, 1)),
            pl.BlockSpec((1, gq.shape[1]), lambda b, i: (0, 0)),
        ],
        out_specs=pl.BlockSpec((MEM_TM, width), lambda b, i: (b * nt + i, 0)),
        compiler_params=_params(("parallel", "arbitrary")),
        name="mem_attn",
    )(z, kvm, kvm, gq)


def _compress_kernel(x_ref, w1_ref, pe_ref, w2_ref, g_ref, o_ref, b_ref):
    rows = o_ref.shape[1]
    is_key = pl.program_id(1) == 0
    ab = [jnp.zeros((rows, 2 * CMP_HIDDEN), F32) for _ in range(2)]
    peb = jnp.zeros((SUBLANES, 2 * CMP_HIDDEN), F32)
    for l in range(CMP_STRIDE):
        x = x_ref[pl.ds(l, rows, stride=CMP_STRIDE), :]
        w = w1_ref[l]
        halves = _split_halves(x)
        for gg in range(2):
            ab[gg] = ab[gg] + _dot(halves[gg][:, :NSA_HEAD_DIM].astype(BF16), w)
        peb = peb + _dot(pe_ref[l], w)
    bias = peb[0:1, :CMP_HIDDEN] + peb[1:2, CMP_HIDDEN:]
    b_ref[pl.ds(rows, SUBLANES), :] = jnp.zeros((SUBLANES, CMP_HIDDEN), F32)
    for gg in range(2):
        b_ref[pl.ds(0, rows), :] = ab[gg][:, CMP_HIDDEN:]
        hid = ab[gg][:, :CMP_HIDDEN] + b_ref[pl.ds(1, rows), :] + bias
        act = (hid * _sigmoid(hid)).astype(BF16)
        out = _dot(act, w2_ref[...])
        ms = jnp.sum(out * out, axis=-1, keepdims=True) / NSA_HEAD_DIM
        normed = out * lax.rsqrt(ms + EPS) * g_ref[...]
        o_ref[gg] = jnp.where(is_key, normed, out).astype(o_ref.dtype)


def _compress(z, w1r, pe8, w2p, gkc, batch, seq):
    rows = seq // CMP_STRIDE
    cb = Z_KV // LANES
    return pl.pallas_call(
        _compress_kernel,
        out_shape=jax.ShapeDtypeStruct((batch, 2, NSA_GROUPS, rows, LANES), BF16),
        grid=(batch, 2, NSA_GROUPS // 2),
        in_specs=[
            pl.BlockSpec((seq, LANES), lambda b, s, p: (b, cb + 2 * s + p)),
            pl.BlockSpec((None, CMP_STRIDE, NSA_HEAD_DIM, 2 * CMP_HIDDEN), lambda b, s, p: (s, 0, 0, 0)),
            pl.BlockSpec((None, CMP_STRIDE, SUBLANES, NSA_HEAD_DIM), lambda b, s, p: (s, 0, 0, 0)),
            pl.BlockSpec((None, CMP_HIDDEN, LANES), lambda b, s, p: (s, 0, 0)),
            pl.BlockSpec((1, LANES), lambda b, s, p: (0, 0)),
        ],
        out_specs=pl.BlockSpec((None, None, 2, rows, LANES), lambda b, s, p: (b, s, p, 0, 0)),
        scratch_shapes=[pltpu.VMEM((rows + SUBLANES, CMP_HIDDEN), F32)],
        compiler_params=_params(("parallel", "arbitrary", "arbitrary")),
        name="compress",
    )(z, w1r, pe8, w2p, gkc)


def _kv_pack_kernel(ks_ref, vs_ref, kw_ref, vw_ref, gks_ref, gkw_ref, ka_ref, vs_o, kw_o, vw_o):
    ts = ks_ref.shape[0]
    ksn = _split_halves(_half_norm(ks_ref[...], gks_ref[...]))
    kwn = _split_halves(_half_norm(kw_ref[...], gkw_ref[...]))
    vsh = _split_halves(vs_ref[...])
    vwh = _split_halves(vw_ref[...])
    pos = pl.program_id(2) * ts + lax.broadcasted_iota(jnp.int32, (ts, LANES), 0)
    blk = lax.broadcasted_iota(jnp.int32, (ts, LANES), 1)
    onehot = jnp.where(pos // SLC_LEN == blk, 1.0, 0.0)
    for gg in range(2):
        ka_ref[gg] = jnp.concatenate([ksn[gg], onehot], axis=1).astype(ka_ref.dtype)
        vs_o[gg] = vsh[gg].astype(vs_o.dtype)
        kw_o[gg] = kwn[gg].astype(kw_o.dtype)
        vw_o[gg] = vwh[gg].astype(vw_o.dtype)


def _kv_pack(z, gks, gkw, batch, seq):
    nst = seq // PACK_TS
    cb = Z_KV // LANES + 4

    def zspec(k):
        return pl.BlockSpec((PACK_TS, LANES), lambda b, p, i: (b * nst + i, cb + 2 * k + p))

    def ospec(w):
        return pl.BlockSpec((None, 2, PACK_TS, w), lambda b, p, i: (b, p, i, 0))

    def oshape(w):
        return jax.ShapeDtypeStruct((batch, NSA_GROUPS, seq, w), BF16)

    gspec = pl.BlockSpec((1, LANES), lambda b, p, i: (0, 0))
    return pl.pallas_call(
        _kv_pack_kernel,
        out_shape=(oshape(2 * LANES), oshape(LANES), oshape(LANES), oshape(LANES)),
        grid=(batch, NSA_GROUPS // 2, nst),
        in_specs=[zspec(0), zspec(1), zspec(2), zspec(3), gspec, gspec],
        out_specs=(ospec(2 * LANES), ospec(LANES), ospec(LANES), ospec(LANES)),
        compiler_params=_params(("parallel", "arbitrary", "arbitrary")),
        name="kv_pack",
    )(z, z, z, z, gks, gkw)


def _topk_bias(score_t):
    nb = score_t.shape[0]
    chunks = [score_t[SUBLANES * v:SUBLANES * (v + 1), :] for v in range(nb // SUBLANES)]
    sub = lax.broadcasted_iota(jnp.int32, (SUBLANES, score_t.shape[1]), 0)
    cnt = [jnp.zeros(c.shape, jnp.int32) for c in chunks]
    for i in range(nb):
        row = jnp.broadcast_to(score_t[i:i + 1, :], chunks[0].shape)
        for v, c in enumerate(chunks):
            if SUBLANES * (v + 1) <= i:
                beats = row > c
            elif SUBLANES * v > i:
                beats = row >= c
            else:
                r = i - SUBLANES * v
                beats = ((sub < r) & (row > c)) | ((sub > r) & (row >= c))
            cnt[v] = cnt[v] + jnp.where(beats, 1, 0)
    rank = jnp.concatenate(cnt, axis=0)
    return jnp.where(rank < SLC_TOP, 0.0, NEG_INF)


def _nsa_kernel(q_ref, gate_ref, gq_ref, ka_ref, vs_ref, kw_ref, vw_ref, kc_ref, vc_ref, map_ref,
                o_ref, m_ref, l_ref, acc_ref):
    g = pl.program_id(1)
    qb = pl.program_id(2)
    t0 = qb * Q_BLOCK
    rows = NSA_HPG * Q_BLOCK
    n_cmp = kc_ref.shape[0]
    scale = NSA_HEAD_DIM ** -0.5

    heads = []
    for pair in range(NSA_HPG // 2):
        qn = _half_norm(q_ref[:, pair * LANES:(pair + 1) * LANES], gq_ref[...]) * scale
        heads.extend(_split_halves(qn))
    q = jnp.concatenate(heads, axis=0).astype(BF16)
    t = t0 + lax.broadcasted_iota(jnp.int32, (rows, 1), 0) % Q_BLOCK

    s = _dot_nt(q, kc_ref[...])
    cmp_end = lax.broadcasted_iota(jnp.int32, (1, n_cmp), 1) * CMP_STRIDE + (CMP_LEN - 1)
    valid = cmp_end <= t
    s = jnp.where(valid, s, NEG_INF)
    e = jnp.exp(s - jnp.max(s, axis=-1, keepdims=True))
    p = jnp.where(valid, e / jnp.sum(e, axis=-1, keepdims=True), 0.0)
    o_cmp = _dot(p.astype(BF16), vc_ref[...])
    imp = p[0:Q_BLOCK]
    for h in range(1, NSA_HPG):
        imp = imp + p[h * Q_BLOCK:(h + 1) * Q_BLOCK]
    hi = imp.astype(BF16)
    r1 = imp - hi.astype(F32)
    mid = r1.astype(BF16)
    lo = (r1 - mid.astype(F32)).astype(BF16)
    wmap = map_ref[...]
    imp_s = _dot(hi, wmap) + _dot(mid, wmap) + _dot(lo, wmap)
    tq = t0 + lax.broadcasted_iota(jnp.int32, (Q_BLOCK, 1), 0)
    cur = tq // SLC_LEN
    blk = lax.broadcasted_iota(jnp.int32, (1, LANES), 1)
    forced = (blk == 0) | (blk == cur) | (blk == cur - 1)
    score = jnp.where(forced, FORCE_SCORE, jnp.where(blk <= cur, imp_s, NEG_INF))
    bias = _topk_bias(score.T).T.astype(BF16)
    q_aug = jnp.concatenate([q, jnp.concatenate([bias] * NSA_HPG, axis=0)], axis=1)

    m_ref[...] = jnp.full(m_ref.shape, NEG_INF, F32)
    l_ref[...] = jnp.zeros(l_ref.shape, F32)
    acc_ref[...] = jnp.zeros(acc_ref.shape, F32)

    def sweep(kt, causal):
        k0 = pl.multiple_of(kt * SEL_TK, SEL_TK)
        s = _dot_nt(q_aug, ka_ref[pl.ds(k0, SEL_TK), :])
        if causal:
            kpos = k0 + lax.broadcasted_iota(jnp.int32, (1, SEL_TK), 1)
            s = jnp.where(kpos <= t, s, NEG_INF)
        m_prev = m_ref[...]
        m_new = jnp.maximum(m_prev, jnp.max(s, axis=-1, keepdims=True))
        alpha = jnp.exp(m_prev - m_new)
        p = jnp.exp(s - m_new)
        l_ref[...] = alpha * l_ref[...] + jnp.sum(p, axis=-1, keepdims=True)
        acc_ref[...] = alpha * acc_ref[...] + _dot(p.astype(BF16), vs_ref[pl.ds(k0, SEL_TK), :])
        m_ref[...] = m_new

    n_full = t0 // SEL_TK

    def body(kt, carry):
        sweep(kt, False)
        return carry

    lax.fori_loop(0, n_full, body, 0)
    sweep(n_full, True)
    o_sel = acc_ref[...] / l_ref[...]

    span = WIN + Q_BLOCK
    start = pl.multiple_of(jnp.maximum(t0 - WIN, 0), Q_BLOCK)
    s = _dot_nt(q, kw_ref[pl.ds(start, span), :])
    kpos = start + lax.broadcasted_iota(jnp.int32, (1, span), 1)
    valid = (kpos <= t) & (kpos > t - WIN)
    s = jnp.where(valid, s, NEG_INF)
    e = jnp.exp(s - jnp.max(s, axis=-1, keepdims=True))
    p = jnp.where(valid, e / jnp.sum(e, axis=-1, keepdims=True), 0.0)
    o_win = _dot(p.astype(BF16), vw_ref[pl.ds(start, span), :])

    gates = _sigmoid(gate_ref[...])
    lane = lax.broadcasted_iota(jnp.int32, (1, LANES), 1)
    outs = []
    for h in range(NSA_HPG):
        head = g * NSA_HPG + h
        rs = slice(h * Q_BLOCK, (h + 1) * Q_BLOCK)
        o = jnp.zeros((Q_BLOCK, LANES), F32)
        for r, branch in enumerate((o_cmp, o_sel, o_win)):
            gcol = jnp.sum(jnp.where(lane == r * NSA_HEADS + head, gates, 0.0), axis=-1, keepdims=True)
            o = o + gcol * branch[rs]
        outs.append(o)
    pairs = [outs[2 * k] + pltpu.roll(outs[2 * k + 1], NSA_HEAD_DIM, axis=1) for k in range(NSA_HPG // 2)]
    o_ref[...] = jnp.concatenate(pairs, axis=1).astype(o_ref.dtype)


def _nsa(z, gq, kaug, vsel, kwin, vwin, kvc, wmap, batch, seq):
    nq = seq // Q_BLOCK
    gw = NSA_HPG * NSA_HEAD_DIM
    rows = NSA_HPG * Q_BLOCK
    n_cmp = seq // CMP_STRIDE

    def kvspec(w):
        return pl.BlockSpec((None, None, seq, w), lambda b, g, i: (b, g, 0, 0))

    def cspec(s):
        return pl.BlockSpec((None, None, None, n_cmp, LANES), lambda b, g, i: (b, s, g, 0, 0))

    return pl.pallas_call(
        _nsa_kernel,
        out_shape=jax.ShapeDtypeStruct((batch * seq, NSA_HEADS * NSA_HEAD_DIM), BF16),
        grid=(batch, NSA_GROUPS, nq),
        in_specs=[
            pl.BlockSpec((Q_BLOCK, gw), lambda b, g, i: (b * nq + i, Z_QNSA // gw + g)),
            pl.BlockSpec((Q_BLOCK, LANES), lambda b, g, i: (b * nq + i, Z_GATE // LANES)),
            pl.BlockSpec((1, LANES), lambda b, g, i: (0, 0)),
            kvspec(2 * LANES), kvspec(LANES), kvspec(LANES), kvspec(LANES),
            cspec(0), cspec(1),
            pl.BlockSpec((n_cmp, LANES), lambda b, g, i: (0, 0)),
        ],
        out_specs=pl.BlockSpec((Q_BLOCK, gw), lambda b, g, i: (b * nq + i, g)),
        scratch_shapes=[pltpu.VMEM((rows, 1), F32), pltpu.VMEM((rows, 1), F32), pltpu.VMEM((rows, LANES), F32)],
        compiler_params=_params(("parallel", "parallel", "arbitrary")),
        name="nsa",
    )(z, z, gq, kaug, vsel, kwin, vwin, kvc, kvc, wmap)


def _selection_weights(seq):
    n_c = (seq - CMP_LEN) // CMP_STRIDE + 1
    n_s = seq // SLC_LEN
    ratio = SLC_LEN // CMP_STRIDE
    w = np.zeros((seq // CMP_STRIDE, LANES), np.float32)
    for j in range(n_s):
        for off in range(-(CMP_LEN // CMP_STRIDE - 1), ratio):
            ci = ratio * j + off
            if 0 <= ci < n_c:
                cs = ci * CMP_STRIDE
                ov = min(cs + CMP_LEN, (j + 1) * SLC_LEN) - max(cs, j * SLC_LEN)
                w[ci, j] = max(ov, 0) / CMP_LEN
    return jnp.asarray(w, BF16)


def _merge_kernel(x_ref, xo_ref, g_ref, yc_ref, yn_ref, ym_ref, wg_ref, wb_ref, wo_ref, o_ref, h_ref, mg_ref, *,
                  n_chunks):
    j = pl.program_id(1)

    @pl.when(j == 0)
    def _():
        h_ref[...] = _rms(x_ref[...], g_ref[...]).astype(BF16)

    @pl.when(j < n_chunks)
    def _():
        h = h_ref[...]
        merged = jnp.zeros((h.shape[0], wg_ref.shape[2]), F32)
        for n, y_ref in enumerate((yc_ref, yn_ref, ym_ref)):
            merged = merged + _sigmoid(_dot(h, wg_ref[n])) * _dot(y_ref[...], wb_ref[n])
        mg_ref[j] = merged.astype(BF16)

    @pl.when(j >= n_chunks)
    def _():
        acc = jnp.zeros(o_ref.shape, F32)
        for c in range(n_chunks):
            acc = acc + _dot(mg_ref[c], wo_ref[c])
        o_ref[...] = xo_ref[...] + acc


def _merge(x, g, yc, yn, ym, wg, wb, wo):
    n, d = x.shape
    bw = yc.shape[1]
    nc = d // MERGE_TN
    wo_c = wo.reshape(nc, MERGE_TN, d)
    return pl.pallas_call(
        functools.partial(_merge_kernel, n_chunks=nc),
        out_shape=jax.ShapeDtypeStruct((n, d), F32),
        grid=(n // MERGE_TM, 2 * nc),
        in_specs=[
            pl.BlockSpec((MERGE_TM, d), lambda i, j: (i, 0)),
            pl.BlockSpec((MERGE_TM, MERGE_TN), lambda i, j: (i, jnp.maximum(j - nc, 0))),
            pl.BlockSpec((1, d), lambda i, j: (0, 0)),
            pl.BlockSpec((MERGE_TM, bw), lambda i, j: (i, 0)),
            pl.BlockSpec((MERGE_TM, bw), lambda i, j: (i, 0)),
            pl.BlockSpec((MERGE_TM, bw), lambda i, j: (i, 0)),
            pl.BlockSpec((N_BRANCH, d, MERGE_TN), lambda i, j: (0, 0, jnp.minimum(j, nc - 1))),
            pl.BlockSpec((N_BRANCH, bw, MERGE_TN), lambda i, j: (0, 0, jnp.minimum(j, nc - 1))),
            pl.BlockSpec((nc, MERGE_TN, MERGE_TN), lambda i, j: (0, 0, jnp.maximum(j - nc, 0))),
        ],
        out_specs=pl.BlockSpec((MERGE_TM, MERGE_TN), lambda i, j: (i, jnp.maximum(j - nc, 0))),
        scratch_shapes=[pltpu.VMEM((MERGE_TM, d), BF16), pltpu.VMEM((nc, MERGE_TM, MERGE_TN), BF16)],
        compiler_params=_params(("parallel", "arbitrary")),
        name="merge",
    )(x, x, g, yc, yn, ym, wg, wb, wo_c)


def _layer(x2d, mem2d, batch, seq, mem_len, p):
    d = x2d.shape[1]
    bw = d // 2
    kvw = NSA_GROUPS * NSA_HEAD_DIM

    w_in = p['w_in']
    o_q = 3 * bw
    o_kv = o_q + NSA_HEADS * NSA_HEAD_DIM
    o_g = o_kv + 6 * kvw
    o_qm = o_g + 3 * NSA_HEADS
    o_gm = o_qm + bw
    gate_cols = w_in[:, o_g:o_qm].reshape(d, NSA_HEADS, 3).transpose(0, 2, 1).reshape(d, 3 * NSA_HEADS)
    w_proj = jnp.concatenate([
        w_in[:, :o_q], w_in[:, o_q:o_kv], w_in[:, o_qm:o_gm], w_in[:, o_kv:o_g], gate_cols,
        jnp.zeros((d, Z_COLS - Z_GATE - 3 * NSA_HEADS), F32)], axis=1).astype(BF16)
    w_gm = w_in[:, o_gm:].reshape(d, N_BRANCH, d).transpose(1, 0, 2).astype(BF16)

    def cmp_weights(w1, w2, pe):
        w1l = w1.reshape(CMP_LEN, NSA_HEAD_DIM, CMP_HIDDEN)
        w1r = jnp.concatenate([w1l[:CMP_STRIDE], w1l[CMP_STRIDE:]], axis=2)
        pe8 = jnp.zeros((CMP_STRIDE, SUBLANES, NSA_HEAD_DIM), F32)
        pe8 = pe8.at[:, 0].set(pe[:CMP_STRIDE]).at[:, 1].set(pe[CMP_STRIDE:])
        w2p = jnp.pad(w2, ((0, 0), (0, LANES - NSA_HEAD_DIM)))
        return w1r.astype(BF16), pe8.astype(BF16), w2p.astype(BF16)

    ck = cmp_weights(p['cmp_w1_k'], p['cmp_w2_k'], p['cmp_pe_k'])
    cv = cmp_weights(p['cmp_w1_v'], p['cmp_w2_v'], p['cmp_pe_v'])
    w1r, pe8, w2p = (jnp.stack([a, b]) for a, b in zip(ck, cv))

    def row(v):
        return v.reshape(1, -1)

    def twice(v):
        return jnp.concatenate([v, v]).reshape(1, LANES)

    gkc = jnp.pad(p['nsa_kc_norm'], (0, LANES - NSA_HEAD_DIM)).reshape(1, LANES)

    x1 = _ffn(x2d, row(p['ffn1_norm']), p['ffn1_w_gate'].astype(BF16), p['ffn1_w_up'].astype(BF16),
              p['ffn1_w_down'].astype(BF16))
    z = _in_proj(x1, row(p['mix_norm']), w_proj)
    y_conv = _conv(z, p['conv_w'], batch, seq)
    kvm = _mem_kv(mem2d, row(p['mem_norm']), p['w_mem_kv'].astype(BF16), row(p['mem_k_norm']))
    y_mem = _mem_attn(z, kvm, row(p['mem_q_norm']), batch, seq, mem_len)
    kvc = _compress(z, w1r, pe8, w2p, gkc, batch, seq)
    kaug, vsel, kwin, vwin = _kv_pack(z, twice(p['nsa_ks_norm']), twice(p['nsa_kw_norm']), batch, seq)
    y_nsa = _nsa(z, twice(p['nsa_q_norm']), kaug, vsel, kwin, vwin, kvc, _selection_weights(seq), batch, seq)
    x2 = _merge(x1, row(p['mix_norm']), y_conv, y_nsa, y_mem, w_gm, p['w_branch'].astype(BF16),
                p['w_o'].astype(BF16))
    return _ffn(x2, row(p['ffn2_norm']), p['ffn2_w_gate'].astype(BF16), p['ffn2_w_up'].astype(BF16),
                p['ffn2_w_down'].astype(BF16))


def kernel(x, mem, ffn1_norm, ffn1_w_gate, ffn1_w_up, ffn1_w_down, mix_norm, mem_norm, w_in, conv_w, nsa_q_norm, nsa_kc_norm, nsa_ks_norm, nsa_kw_norm, cmp_pe_k, cmp_w1_k, cmp_w2_k, cmp_pe_v, cmp_w1_v, cmp_w2_v, w_mem_kv, mem_q_norm, mem_k_norm, w_branch, w_o, ffn2_norm, ffn2_w_gate, ffn2_w_up, ffn2_w_down):
    batch, seq, d = x.shape
    mem_len = mem.shape[1]
    assert seq % (SEL_TK * 4) == 0 and seq // SLC_LEN <= LANES and seq >= WIN + Q_BLOCK
    params = dict(
        ffn1_norm=ffn1_norm, ffn1_w_gate=ffn1_w_gate, ffn1_w_up=ffn1_w_up, ffn1_w_down=ffn1_w_down,
        mix_norm=mix_norm, mem_norm=mem_norm, w_in=w_in, conv_w=conv_w, nsa_q_norm=nsa_q_norm,
        nsa_kc_norm=nsa_kc_norm, nsa_ks_norm=nsa_ks_norm, nsa_kw_norm=nsa_kw_norm, cmp_pe_k=cmp_pe_k,
        cmp_w1_k=cmp_w1_k, cmp_w2_k=cmp_w2_k, cmp_pe_v=cmp_pe_v, cmp_w1_v=cmp_w1_v, cmp_w2_v=cmp_w2_v,
        w_mem_kv=w_mem_kv, mem_q_norm=mem_q_norm, mem_k_norm=mem_k_norm, w_branch=w_branch, w_o=w_o,
        ffn2_norm=ffn2_norm, ffn2_w_gate=ffn2_w_gate, ffn2_w_up=ffn2_w_up, ffn2_w_down=ffn2_w_down)
    x2d = x.reshape(batch * seq, d)
    mem2d = mem.reshape(batch * mem_len, d)
    for l in range(ffn1_norm.shape[0]):
        x2d = _layer(x2d, mem2d, batch, seq, mem_len, {k: v[l] for k, v in params.items()})
    return x2d.reshape(batch, seq, d)
```

```python
import functools
import math

import jax
import jax.numpy as jnp
import numpy as np
from jax import lax
from jax.experimental import pallas as pl
from jax.experimental.pallas import tpu as pltpu

F32 = jnp.float32
BF16 = jnp.bfloat16

EPS = 1e-6
NEG_INF = -1e30
FORCE_SCORE = 1e9

CONV_K = 3
NSA_HEAD_DIM = 64
NSA_GROUPS = 4
NSA_HPG = 4
NSA_HEADS = NSA_GROUPS * NSA_HPG
CMP_LEN = 32
CMP_STRIDE = 16
CMP_HIDDEN = 4 * NSA_HEAD_DIM
SLC_LEN = 64
SLC_TOP = 16
WIN = 512
Q_BLOCK = 128
MEM_HEADS = 4
N_BRANCH = 3

LANES = 128
SUBLANES = 8
VMEM_LIMIT_BYTES = 56 * 1024 * 1024

FFN_TM = 512
FFN_TF = 512
PROJ_TM = 512
PROJ_TN = 768
CONV_TS = 512
MEM_TM = 512
PACK_TS = 1024
SEL_TK = 512
MERGE_TM = 512
MERGE_TN = 512

Z_CONV = 0
Z_QNSA = 3072
Z_QMEM = 4096
Z_KV = 5120
Z_GATE = 6656
Z_COLS = 6912


def _params(sem):
    return pltpu.CompilerParams(dimension_semantics=sem, vmem_limit_bytes=VMEM_LIMIT_BYTES)


def _rms(x, g):
    return x * lax.rsqrt(jnp.mean(x * x, axis=-1, keepdims=True) + EPS) * g


def _sigmoid(x):
    return 1.0 / (1.0 + jnp.exp(-x))


def _dot(a, b):
    return jnp.dot(a, b, preferred_element_type=F32)


def _dot_nt(a, b):
    return lax.dot_general(a, b, (((1,), (1,)), ((), ())), preferred_element_type=F32)


def _half_norm(x, g):
    lane = lax.broadcasted_iota(jnp.int32, x.shape, 1)
    lo = lane < NSA_HEAD_DIM
    xx = x * x
    s_lo = jnp.sum(jnp.where(lo, xx, 0.0), axis=-1, keepdims=True)
    s_hi = jnp.sum(jnp.where(lo, 0.0, xx), axis=-1, keepdims=True)
    r = jnp.where(lo, lax.rsqrt(s_lo / NSA_HEAD_DIM + EPS), lax.rsqrt(s_hi / NSA_HEAD_DIM + EPS))
    return x * r * g


def _split_halves(x):
    lane = lax.broadcasted_iota(jnp.int32, x.shape, 1)
    lo = lane < NSA_HEAD_DIM
    return jnp.where(lo, x, 0.0), jnp.where(lo, pltpu.roll(x, NSA_HEAD_DIM, axis=1), 0.0)


def _ffn_kernel(x_ref, g_ref, wg_ref, wu_ref, wd_ref, o_ref, h_ref):
    j = pl.program_id(1)

    @pl.when(j == 0)
    def _():
        h_ref[...] = _rms(x_ref[...], g_ref[...]).astype(BF16)
        o_ref[...] = jnp.zeros_like(o_ref)

    h = h_ref[...]
    gate = _dot(h, wg_ref[...])
    up = _dot(h, wu_ref[...])
    act = (gate * _sigmoid(gate) * up).astype(BF16)
    o_ref[...] += _dot(act, wd_ref[...])

    @pl.when(j == pl.num_programs(1) - 1)
    def _():
        o_ref[...] = x_ref[...] + 0.5 * o_ref[...]


def _ffn(x, g, wg, wu, wd):
    n, d = x.shape
    f = wg.shape[1]
    return pl.pallas_call(
        _ffn_kernel,
        out_shape=jax.ShapeDtypeStruct((n, d), F32),
        grid=(n // FFN_TM, f // FFN_TF),
        in_specs=[
            pl.BlockSpec((FFN_TM, d), lambda i, j: (i, 0)),
            pl.BlockSpec((1, d), lambda i, j: (0, 0)),
            pl.BlockSpec((d, FFN_TF), lambda i, j: (0, j)),
            pl.BlockSpec((d, FFN_TF), lambda i, j: (0, j)),
            pl.BlockSpec((FFN_TF, d), lambda i, j: (j, 0)),
        ],
        out_specs=pl.BlockSpec((FFN_TM, d), lambda i, j: (i, 0)),
        scratch_shapes=[pltpu.VMEM((FFN_TM, d), BF16)],
        compiler_params=_params(("parallel", "arbitrary")),
        name="ffn",
    )(x, g, wg, wu, wd)


def _proj_kernel(x_ref, g_ref, w_ref, o_ref, h_ref):
    @pl.when(pl.program_id(1) == 0)
    def _():
        h_ref[...] = _rms(x_ref[...], g_ref[...]).astype(BF16)

    o_ref[...] = _dot(h_ref[...], w_ref[...])


def _in_proj(x, g, w):
    n, d = x.shape
    c = w.shape[1]
    return pl.pallas_call(
        _proj_kernel,
        out_shape=jax.ShapeDtypeStruct((n, c), F32),
        grid=(n // PROJ_TM, c // PROJ_TN),
        in_specs=[
            pl.BlockSpec((PROJ_TM, d), lambda i, j: (i, 0)),
            pl.BlockSpec((1, d), lambda i, j: (0, 0)),
            pl.BlockSpec((d, PROJ_TN), lambda i, j: (0, j)),
        ],
        out_specs=pl.BlockSpec((PROJ_TM, PROJ_TN), lambda i, j: (i, j)),
        scratch_shapes=[pltpu.VMEM((PROJ_TM, d), BF16)],
        compiler_params=_params(("parallel", "arbitrary")),
        name="in_proj",
    )(x, g, w)


def _conv_kernel(b_ref, c_ref, u_ref, cp_ref, up_ref, w_ref, o_ref, z_ref):
    ts = b_ref.shape[0]
    first = pl.program_id(1) == 0
    zp = cp_ref[...] * up_ref[...]
    z_ref[pl.ds(0, SUBLANES), :] = jnp.where(first, 0.0, zp)
    z = c_ref[...] * u_ref[...]
    z_ref[pl.ds(SUBLANES, ts), :] = z
    z1 = z_ref[pl.ds(SUBLANES - 1, ts), :]
    z2 = z_ref[pl.ds(SUBLANES - 2, ts), :]
    w = w_ref[...]
    y = b_ref[...] * (w[0:1, :] * z2 + w[1:2, :] * z1 + w[2:3, :] * z)
    o_ref[...] = y.astype(o_ref.dtype)


def _conv(z, conv_w, batch, seq):
    cw = conv_w.shape[1]
    nst = seq // CONV_TS
    rows_per = CONV_TS // SUBLANES
    cb = Z_CONV // cw

    def prev_map(col):
        return lambda b, i: (jnp.maximum((b * nst + i) * rows_per - 1, 0), col)

    return pl.pallas_call(
        _conv_kernel,
        out_shape=jax.ShapeDtypeStruct((batch * seq, cw), BF16),
        grid=(batch, nst),
        in_specs=[
            pl.BlockSpec((CONV_TS, cw), lambda b, i: (b * nst + i, cb)),
            pl.BlockSpec((CONV_TS, cw), lambda b, i: (b * nst + i, cb + 1)),
            pl.BlockSpec((CONV_TS, cw), lambda b, i: (b * nst + i, cb + 2)),
            pl.BlockSpec((SUBLANES, cw), prev_map(cb + 1)),
            pl.BlockSpec((SUBLANES, cw), prev_map(cb + 2)),
            pl.BlockSpec((CONV_K, cw), lambda b, i: (0, 0)),
        ],
        out_specs=pl.BlockSpec((CONV_TS, cw), lambda b, i: (b * nst + i, 0)),
        scratch_shapes=[pltpu.VMEM((CONV_TS + SUBLANES, cw), F32)],
        compiler_params=_params(("parallel", "arbitrary")),
        name="conv",
    )(z, z, z, z, z, conv_w)


def _mem_kv_kernel(m_ref, g_ref, w_ref, gk_ref, o_ref, *, n_key_tiles):
    h = _rms(m_ref[...], g_ref[...]).astype(BF16)
    kv = _dot(h, w_ref[...])
    is_key = pl.program_id(0) < n_key_tiles
    o_ref[...] = jnp.where(is_key, _rms(kv, gk_ref[...]), kv).astype(o_ref.dtype)


def _mem_kv(mem2d, g, w, gk):
    rows, d = mem2d.shape
    hd = gk.shape[1]
    cols = w.shape[1]
    return pl.pallas_call(
        functools.partial(_mem_kv_kernel, n_key_tiles=cols // (2 * hd)),
        out_shape=jax.ShapeDtypeStruct((rows, cols), BF16),
        grid=(cols // hd,),
        in_specs=[
            pl.BlockSpec((rows, d), lambda j: (0, 0)),
            pl.BlockSpec((1, d), lambda j: (0, 0)),
            pl.BlockSpec((d, hd), lambda j: (0, j)),
            pl.BlockSpec((1, hd), lambda j: (0, 0)),
        ],
        out_specs=pl.BlockSpec((rows, hd), lambda j: (0, j)),
        compiler_params=_params(("arbitrary",)),
        name="mem_kv",
    )(mem2d, g, w, gk)


def _mem_attn_kernel(q_ref, k_ref, v_ref, gq_ref, o_ref):
    hd = gq_ref.shape[1]
    scale = hd ** -0.5
    outs = []
    for h in range(MEM_HEADS):
        sl = slice(h * hd, (h + 1) * hd)
        q = (_rms(q_ref[:, sl], gq_ref[...]) * scale).astype(BF16)
        s = _dot_nt(q, k_ref[:, sl])
        e = jnp.exp(s - jnp.max(s, axis=-1, keepdims=True))
        p = e / jnp.sum(e, axis=-1, keepdims=True)
        outs.append(_dot(p.astype(BF16), v_ref[:, sl]))
    o_ref[...] = jnp.concatenate(outs, axis=1).astype(o_ref.dtype)


def _mem_attn(z, kvm, gq, batch, seq, mem_len):
    width = MEM_HEADS * gq.shape[1]
    nt = seq // MEM_TM
    return pl.pallas_call(
        _mem_attn_kernel,
        out_shape=jax.ShapeDtypeStruct((batch * seq, width), BF16),
        grid=(batch, nt),
        in_specs=[
            pl.BlockSpec((MEM_TM, width), lambda b, i: (b * nt + i, Z_QMEM // width)),
            pl.BlockSpec((mem_len, width), lambda b, i: (b, 0)),
            pl.BlockSpec((mem_len, width), lambda b, i: (b, 1)),
            pl.BlockSpec((1, gq.shape[1]), lambda b, i: (0, 0)),
        ],
        out_specs=pl.BlockSpec((MEM_TM, width), lambda b, i: (b * nt + i, 0)),
        compiler_params=_params(("parallel", "arbitrary")),
        name="mem_attn",
    )(z, kvm, kvm, gq)


def _compress_kernel(x_ref, w1_ref, pe_ref, w2_ref, g_ref, o_ref, b_ref):
    rows = o_ref.shape[1]
    is_key = pl.program_id(1) == 0
    ab = [jnp.zeros((rows, 2 * CMP_HIDDEN), F32) for _ in range(2)]
    peb = jnp.zeros((SUBLANES, 2 * CMP_HIDDEN), F32)
    for l in range(CMP_STRIDE):
        x = x_ref[pl.ds(l, rows, stride=CMP_STRIDE), :]
        w = w1_ref[l]
        halves = _split_halves(x)
        for gg in range(2):
            ab[gg] = ab[gg] + _dot(halves[gg][:, :NSA_HEAD_DIM].astype(BF16), w)
        peb = peb + _dot(pe_ref[l], w)
    bias = peb[0:1, :CMP_HIDDEN] + peb[1:2, CMP_HIDDEN:]
    b_ref[pl.ds(rows, SUBLANES), :] = jnp.zeros((SUBLANES, CMP_HIDDEN), F32)
    for gg in range(2):
        b_ref[pl.ds(0, rows), :] = ab[gg][:, CMP_HIDDEN:]
        hid = ab[gg][:, :CMP_HIDDEN] + b_ref[pl.ds(1, rows), :] + bias
        act = (hid * _sigmoid(hid)).astype(BF16)
        out = _dot(act, w2_ref[...])
        ms = jnp.sum(out * out, axis=-1, keepdims=True) / NSA_HEAD_DIM
        normed = out * lax.rsqrt(ms + EPS) * g_ref[...]
        o_ref[gg] = jnp.where(is_key, normed, out).astype(o_ref.dtype)


def _compress(z, w1r, pe8, w2p, gkc, batch, seq):
    rows = seq // CMP_STRIDE
    cb = Z_KV // LANES
    return pl.pallas_call(
        _compress_kernel,
        out_shape=jax.ShapeDtypeStruct((batch, 2, NSA_GROUPS, rows, LANES), BF16),
        grid=(batch, 2, NSA_GROUPS // 2),
        in_specs=[
            pl.BlockSpec((seq, LANES), lambda b, s, p: (b, cb + 2 * s + p)),
            pl.BlockSpec((None, CMP_STRIDE, NSA_HEAD_DIM, 2 * CMP_HIDDEN), lambda b, s, p: (s, 0, 0, 0)),
            pl.BlockSpec((None, CMP_STRIDE, SUBLANES, NSA_HEAD_DIM), lambda b, s, p: (s, 0, 0, 0)),
            pl.BlockSpec((None, CMP_HIDDEN, LANES), lambda b, s, p: (s, 0, 0)),
            pl.BlockSpec((1, LANES), lambda b, s, p: (0, 0)),
        ],
        out_specs=pl.BlockSpec((None, None, 2, rows, LANES), lambda b, s, p: (b, s, p, 0, 0)),
        scratch_shapes=[pltpu.VMEM((rows + SUBLANES, CMP_HIDDEN), F32)],
        compiler_params=_params(("parallel", "arbitrary", "arbitrary")),
        name="compress",
    )(z, w1r, pe8, w2p, gkc)


def _kv_pack_kernel(ks_ref, vs_ref, kw_ref, vw_ref, gks_ref, gkw_ref, ka_ref, vs_o, kw_o, vw_o):
    ts = ks_ref.shape[0]
    ksn = _split_halves(_half_norm(ks_ref[...], gks_ref[...]))
    kwn = _split_halves(_half_norm(kw_ref[...], gkw_ref[...]))
    vsh = _split_halves(vs_ref[...])
    vwh = _split_halves(vw_ref[...])
    pos = pl.program_id(2) * ts + lax.broadcasted_iota(jnp.int32, (ts, LANES), 0)
    blk = lax.broadcasted_iota(jnp.int32, (ts, LANES), 1)
    onehot = jnp.where(pos // SLC_LEN == blk, 1.0, 0.0)
    ones_col = jnp.where(blk == NSA_HEAD_DIM, 1.0, 0.0)
    for gg in range(2):
        ka_ref[gg] = jnp.concatenate([ksn[gg], onehot], axis=1).astype(ka_ref.dtype)
        vs_o[gg] = (vsh[gg] + ones_col).astype(vs_o.dtype)
        kw_o[gg] = kwn[gg].astype(kw_o.dtype)
        vw_o[gg] = vwh[gg].astype(vw_o.dtype)


def _kv_pack(z, gks, gkw, batch, seq):
    nst = seq // PACK_TS
    cb = Z_KV // LANES + 4

    def zspec(k):
        return pl.BlockSpec((PACK_TS, LANES), lambda b, p, i: (b * nst + i, cb + 2 * k + p))

    def ospec(w):
        return pl.BlockSpec((None, 2, PACK_TS, w), lambda b, p, i: (b, p, i, 0))

    def oshape(w):
        return jax.ShapeDtypeStruct((batch, NSA_GROUPS, seq, w), BF16)

    gspec = pl.BlockSpec((1, LANES), lambda b, p, i: (0, 0))
    return pl.pallas_call(
        _kv_pack_kernel,
        out_shape=(oshape(2 * LANES), oshape(LANES), oshape(LANES), oshape(LANES)),
        grid=(batch, NSA_GROUPS // 2, nst),
        in_specs=[zspec(0), zspec(1), zspec(2), zspec(3), gspec, gspec],
        out_specs=(ospec(2 * LANES), ospec(LANES), ospec(LANES), ospec(LANES)),
        compiler_params=_params(("parallel", "arbitrary", "arbitrary")),
        name="kv_pack",
    )(z, z, z, z, gks, gkw)


def _count_outranking(score_ref, cnt_ref, n_live):
    nb, nq = score_ref.shape
    n_chunks = nb // SUBLANES
    cnt_ref[...] = jnp.zeros(cnt_ref.shape, cnt_ref.dtype)
    for ci in range(n_chunks):
        @pl.when(ci < n_live)
        def _():
            sub = lax.broadcasted_iota(jnp.int32, (SUBLANES, nq), 0)
            chunks = [score_ref[SUBLANES * v:SUBLANES * (v + 1), :] for v in range(n_chunks)]
            cnt = [cnt_ref[SUBLANES * v:SUBLANES * (v + 1), :] for v in range(n_chunks)]
            for r in range(SUBLANES):
                row = jnp.broadcast_to(chunks[ci][r:r + 1, :], (SUBLANES, nq))
                for v, c in enumerate(chunks):
                    if v < ci:
                        beats = row > c
                    elif v > ci:
                        beats = row >= c
                    else:
                        beats = ((sub < r) & (row > c)) | ((sub > r) & (row >= c))
                    cnt[v] = cnt[v] + jnp.where(beats, 1, 0)
            for v in range(n_chunks):
                cnt_ref[SUBLANES * v:SUBLANES * (v + 1), :] = cnt[v]


def _nsa_kernel(q_ref, gate_ref, gq_ref, ka_ref, vs_ref, kw_ref, vw_ref, kc_ref, vc_ref, map_ref,
                o_ref, m_ref, acc_ref, sa_ref, sb_ref, mta_ref, mtb_ref, qa_ref, score_ref, cnt_ref, oc_ref, ow_ref):
    g = pl.program_id(1)
    qb = pl.program_id(2)
    t0 = qb * Q_BLOCK
    rows = NSA_HPG * Q_BLOCK
    n_cmp = kc_ref.shape[0]
    scale = NSA_HEAD_DIM ** -0.5

    heads = []
    for pair in range(NSA_HPG // 2):
        qn = _half_norm(q_ref[:, pair * LANES:(pair + 1) * LANES], gq_ref[...]) * scale
        heads.extend(_split_halves(qn))
    q = jnp.concatenate(heads, axis=0).astype(BF16)
    t = t0 + lax.broadcasted_iota(jnp.int32, (rows, 1), 0) % Q_BLOCK

    span = WIN + Q_BLOCK
    start = pl.multiple_of(jnp.maximum(t0 - WIN, 0), Q_BLOCK)
    s = _dot_nt(q, kw_ref[pl.ds(start, span), :])
    kpos = start + lax.broadcasted_iota(jnp.int32, (1, span), 1)
    s = jnp.where((kpos <= t) & (kpos > t - WIN), s, NEG_INF)
    e = jnp.exp(s - jnp.max(s, axis=-1, keepdims=True))
    p = e / jnp.sum(e, axis=-1, keepdims=True)
    ow_ref[...] = _dot(p.astype(BF16), vw_ref[pl.ds(start, span), :])

    s = _dot_nt(q, kc_ref[...])
    cmp_end = lax.broadcasted_iota(jnp.int32, (1, n_cmp), 1) * CMP_STRIDE + (CMP_LEN - 1)
    valid = cmp_end <= t
    s = jnp.where(valid, s, NEG_INF)
    e = jnp.exp(s - jnp.maximum(jnp.max(s, axis=-1, keepdims=True), 0.1 * NEG_INF))
    p = e / jnp.maximum(jnp.sum(e, axis=-1, keepdims=True), 1e-30)
    oc_ref[...] = _dot(p.astype(BF16), vc_ref[...])
    imp = p[0:Q_BLOCK]
    for h in range(1, NSA_HPG):
        imp = imp + p[h * Q_BLOCK:(h + 1) * Q_BLOCK]
    hi = imp.astype(BF16)
    r1 = imp - hi.astype(F32)
    mid = r1.astype(BF16)
    lo = (r1 - mid.astype(F32)).astype(BF16)
    wmap = map_ref[...]
    imp_s = _dot(hi, wmap) + _dot(mid, wmap) + _dot(lo, wmap)
    tq = t0 + lax.broadcasted_iota(jnp.int32, (Q_BLOCK, 1), 0)
    cur = tq // SLC_LEN
    blk = lax.broadcasted_iota(jnp.int32, (1, LANES), 1)
    forced = (blk == 0) | (blk == cur) | (blk == cur - 1)
    score = jnp.where(forced, FORCE_SCORE, jnp.where(blk <= cur, imp_s, NEG_INF))
    score_ref[...] = score.T
    _count_outranking(score_ref, cnt_ref, (t0 + Q_BLOCK - 1) // SLC_LEN // SUBLANES + 1)
    chosen = jnp.where(cnt_ref[...] < SLC_TOP, 0.0, NEG_INF).T
    bias = jnp.where(blk <= cur, chosen, NEG_INF).astype(BF16)
    qa_ref[...] = jnp.concatenate([q, jnp.concatenate([bias] * NSA_HPG, axis=0)], axis=1)

    n_cols = SEL_TK // LANES
    m_ref[...] = jnp.full(m_ref.shape, NEG_INF, F32)
    acc_ref[...] = jnp.zeros(acc_ref.shape, F32)

    def tile_max(s_ref):
        mx = s_ref[0]
        for c in range(1, n_cols):
            mx = jnp.maximum(mx, s_ref[c])
        return jnp.broadcast_to(jnp.max(mx, axis=-1, keepdims=True), mx.shape)

    def scores(kt, s_ref, mt_ref):
        k0 = pl.multiple_of(kt * SEL_TK, SEL_TK)
        s = _dot_nt(qa_ref[...], ka_ref[pl.ds(k0, SEL_TK), :])
        for c in range(n_cols):
            s_ref[c] = s[:, c * LANES:(c + 1) * LANES]
        mt_ref[...] = tile_max(s_ref)

    def update(kt, s_ref, mt_ref):
        k0 = pl.multiple_of(kt * SEL_TK, SEL_TK)
        m_prev = m_ref[...]
        m_new = jnp.maximum(m_prev, mt_ref[...])
        alpha = jnp.exp(m_prev - m_new)
        p = jnp.concatenate([jnp.exp(s_ref[c] - m_new) for c in range(n_cols)], axis=1)
        acc_ref[...] = alpha * acc_ref[...] + _dot(p.astype(BF16), vs_ref[pl.ds(k0, SEL_TK), :])
        m_ref[...] = m_new

    n_full = t0 // SEL_TK
    odd = n_full % 2

    @pl.when(odd == 0)
    def _():
        scores(0, sa_ref, mta_ref)

    @pl.when(odd == 1)
    def _():
        scores(0, sb_ref, mtb_ref)
        scores(1, sa_ref, mta_ref)
        update(0, sb_ref, mtb_ref)

    def pair(j, carry):
        kt = odd + 2 * j
        scores(kt + 1, sb_ref, mtb_ref)
        update(kt, sa_ref, mta_ref)
        scores(kt + 2, sa_ref, mta_ref)
        update(kt + 1, sb_ref, mtb_ref)
        return carry

    lax.fori_loop(0, n_full // 2, pair, 0)
    dcol = qb % n_cols
    kpos = t0 + lax.broadcasted_iota(jnp.int32, (1, LANES), 1)
    sa_ref[dcol] = jnp.where(kpos <= t, sa_ref[dcol], NEG_INF)
    mta_ref[...] = tile_max(sa_ref)
    update(n_full, sa_ref, mta_ref)
    lane_r = lax.broadcasted_iota(jnp.int32, (1, LANES), 1)
    acc = acc_ref[...]
    denom = jnp.sum(jnp.where(lane_r == NSA_HEAD_DIM, acc, 0.0), axis=-1, keepdims=True)
    o_sel = jnp.where(lane_r < NSA_HEAD_DIM, acc / denom, 0.0)

    gates = _sigmoid(gate_ref[...])
    lane = lax.broadcasted_iota(jnp.int32, (1, LANES), 1)
    outs = []
    for h in range(NSA_HPG):
        head = g * NSA_HPG + h
        rs = slice(h * Q_BLOCK, (h + 1) * Q_BLOCK)
        o = jnp.zeros((Q_BLOCK, LANES), F32)
        for r, branch in enumerate((oc_ref[rs, :], o_sel[rs], ow_ref[rs, :])):
            gcol = jnp.sum(jnp.where(lane == r * NSA_HEADS + head, gates, 0.0), axis=-1, keepdims=True)
            o = o + gcol * branch
        outs.append(o)
    pairs = [outs[2 * k] + pltpu.roll(outs[2 * k + 1], NSA_HEAD_DIM, axis=1) for k in range(NSA_HPG // 2)]
    o_ref[...] = jnp.concatenate(pairs, axis=1).astype(o_ref.dtype)


def _nsa(z, gq, kaug, vsel, kwin, vwin, kvc, wmap, batch, seq):
    nq = seq // Q_BLOCK
    gw = NSA_HPG * NSA_HEAD_DIM
    rows = NSA_HPG * Q_BLOCK
    n_cmp = seq // CMP_STRIDE

    def kvspec(w):
        return pl.BlockSpec((None, None, seq, w), lambda b, g, i: (b, g, 0, 0))

    def cspec(s):
        return pl.BlockSpec((None, None, None, n_cmp, LANES), lambda b, g, i: (b, s, g, 0, 0))

    return pl.pallas_call(
        _nsa_kernel,
        out_shape=jax.ShapeDtypeStruct((batch * seq, NSA_HEADS * NSA_HEAD_DIM), BF16),
        grid=(batch, NSA_GROUPS, nq),
        in_specs=[
            pl.BlockSpec((Q_BLOCK, gw), lambda b, g, i: (b * nq + i, Z_QNSA // gw + g)),
            pl.BlockSpec((Q_BLOCK, LANES), lambda b, g, i: (b * nq + i, Z_GATE // LANES)),
            pl.BlockSpec((1, LANES), lambda b, g, i: (0, 0)),
            kvspec(2 * LANES), kvspec(LANES), kvspec(LANES), kvspec(LANES),
            cspec(0), cspec(1),
            pl.BlockSpec((n_cmp, LANES), lambda b, g, i: (0, 0)),
        ],
        out_specs=pl.BlockSpec((Q_BLOCK, gw), lambda b, g, i: (b * nq + i, g)),
        scratch_shapes=[
            pltpu.VMEM((rows, LANES), F32), pltpu.VMEM((rows, LANES), F32),
            pltpu.VMEM((SEL_TK // LANES, rows, LANES), F32), pltpu.VMEM((SEL_TK // LANES, rows, LANES), F32),
            pltpu.VMEM((rows, LANES), F32), pltpu.VMEM((rows, LANES), F32),
            pltpu.VMEM((rows, 2 * LANES), BF16),
            pltpu.VMEM((LANES, Q_BLOCK), F32), pltpu.VMEM((LANES, Q_BLOCK), jnp.int32),
            pltpu.VMEM((rows, LANES), F32), pltpu.VMEM((rows, LANES), F32)],
        compiler_params=_params(("parallel", "parallel", "arbitrary")),
        name="nsa",
    )(z, z, gq, kaug, vsel, kwin, vwin, kvc, kvc, wmap)


def _selection_weights(seq):
    n_c = (seq - CMP_LEN) // CMP_STRIDE + 1
    n_s = seq // SLC_LEN
    ratio = SLC_LEN // CMP_STRIDE
    w = np.zeros((seq // CMP_STRIDE, LANES), np.float32)
    for j in range(n_s):
        for off in range(-(CMP_LEN // CMP_STRIDE - 1), ratio):
            ci = ratio * j + off
            if 0 <= ci < n_c:
                cs = ci * CMP_STRIDE
                ov = min(cs + CMP_LEN, (j + 1) * SLC_LEN) - max(cs, j * SLC_LEN)
                w[ci, j] = max(ov, 0) / CMP_LEN
    return jnp.asarray(w, BF16)


def _merge_kernel(x_ref, xo_ref, g_ref, yc_ref, yn_ref, ym_ref, wg_ref, wb_ref, wo_ref, o_ref, h_ref, mg_ref, *,
                  n_chunks):
    j = pl.program_id(1)

    @pl.when(j == 0)
    def _():
        h_ref[...] = _rms(x_ref[...], g_ref[...]).astype(BF16)

    @pl.when(j < n_chunks)
    def _():
        h = h_ref[...]
        merged = jnp.zeros((h.shape[0], wg_ref.shape[2]), F32)
        for n, y_ref in enumerate((yc_ref, yn_ref, ym_ref)):
            merged = merged + _sigmoid(_dot(h, wg_ref[n])) * _dot(y_ref[...], wb_ref[n])
        mg_ref[j] = merged.astype(BF16)

    @pl.when(j >= n_chunks)
    def _():
        acc = jnp.zeros(o_ref.shape, F32)
        for c in range(n_chunks):
            acc = acc + _dot(mg_ref[c], wo_ref[c])
        o_ref[...] = xo_ref[...] + acc


def _merge(x, g, yc, yn, ym, wg, wb, wo):
    n, d = x.shape
    bw = yc.shape[1]
    nc = d // MERGE_TN
    wo_c = wo.reshape(nc, MERGE_TN, d)
    return pl.pallas_call(
        functools.partial(_merge_kernel, n_chunks=nc),
        out_shape=jax.ShapeDtypeStruct((n, d), F32),
        grid=(n // MERGE_TM, 2 * nc),
        in_specs=[
            pl.BlockSpec((MERGE_TM, d), lambda i, j: (i, 0)),
            pl.BlockSpec((MERGE_TM, MERGE_TN), lambda i, j: (i, jnp.maximum(j - nc, 0))),
            pl.BlockSpec((1, d), lambda i, j: (0, 0)),
            pl.BlockSpec((MERGE_TM, bw), lambda i, j: (i, 0)),
            pl.BlockSpec((MERGE_TM, bw), lambda i, j: (i, 0)),
            pl.BlockSpec((MERGE_TM, bw), lambda i, j: (i, 0)),
            pl.BlockSpec((N_BRANCH, d, MERGE_TN), lambda i, j: (0, 0, jnp.minimum(j, nc - 1))),
            pl.BlockSpec((N_BRANCH, bw, MERGE_TN), lambda i, j: (0, 0, jnp.minimum(j, nc - 1))),
            pl.BlockSpec((nc, MERGE_TN, MERGE_TN), lambda i, j: (0, 0, jnp.maximum(j - nc, 0))),
        ],
        out_specs=pl.BlockSpec((MERGE_TM, MERGE_TN), lambda i, j: (i, jnp.maximum(j - nc, 0))),
        scratch_shapes=[pltpu.VMEM((MERGE_TM, d), BF16), pltpu.VMEM((nc, MERGE_TM, MERGE_TN), BF16)],
        compiler_params=_params(("parallel", "arbitrary")),
        name="merge",
    )(x, x, g, yc, yn, ym, wg, wb, wo_c)


def _layer(x2d, mem2d, batch, seq, mem_len, p):
    d = x2d.shape[1]
    bw = d // 2
    kvw = NSA_GROUPS * NSA_HEAD_DIM

    w_in = p['w_in']
    o_q = 3 * bw
    o_kv = o_q + NSA_HEADS * NSA_HEAD_DIM
    o_g = o_kv + 6 * kvw
    o_qm = o_g + 3 * NSA_HEADS
    o_gm = o_qm + bw
    gate_cols = w_in[:, o_g:o_qm].reshape(d, NSA_HEADS, 3).transpose(0, 2, 1).reshape(d, 3 * NSA_HEADS)
    w_proj = jnp.concatenate([
        w_in[:, :o_q], w_in[:, o_q:o_kv], w_in[:, o_qm:o_gm], w_in[:, o_kv:o_g], gate_cols,
        jnp.zeros((d, Z_COLS - Z_GATE - 3 * NSA_HEADS), F32)], axis=1).astype(BF16)
    w_gm = w_in[:, o_gm:].reshape(d, N_BRANCH, d).transpose(1, 0, 2).astype(BF16)

    def cmp_weights(w1, w2, pe):
        w1l = w1.reshape(CMP_LEN, NSA_HEAD_DIM, CMP_HIDDEN)
        w1r = jnp.concatenate([w1l[:CMP_STRIDE], w1l[CMP_STRIDE:]], axis=2)
        pe8 = jnp.zeros((CMP_STRIDE, SUBLANES, NSA_HEAD_DIM), F32)
        pe8 = pe8.at[:, 0].set(pe[:CMP_STRIDE]).at[:, 1].set(pe[CMP_STRIDE:])
        w2p = jnp.pad(w2, ((0, 0), (0, LANES - NSA_HEAD_DIM)))
        return w1r.astype(BF16), pe8.astype(BF16), w2p.astype(BF16)

    ck = cmp_weights(p['cmp_w1_k'], p['cmp_w2_k'], p['cmp_pe_k'])
    cv = cmp_weights(p['cmp_w1_v'], p['cmp_w2_v'], p['cmp_pe_v'])
    w1r, pe8, w2p = (jnp.stack([a, b]) for a, b in zip(ck, cv))

    def row(v):
        return v.reshape(1, -1)

    def twice(v):
        return jnp.concatenate([v, v]).reshape(1, LANES)

    gkc = jnp.pad(p['nsa_kc_norm'], (0, LANES - NSA_HEAD_DIM)).reshape(1, LANES)

    x1 = _ffn(x2d, row(p['ffn1_norm']), p['ffn1_w_gate'].astype(BF16), p['ffn1_w_up'].astype(BF16),
              p['ffn1_w_down'].astype(BF16))
    z = _in_proj(x1, row(p['mix_norm']), w_proj)
    y_conv = _conv(z, p['conv_w'], batch, seq)
    kvm = _mem_kv(mem2d, row(p['mem_norm']), p['w_mem_kv'].astype(BF16), row(p['mem_k_norm']))
    y_mem = _mem_attn(z, kvm, row(p['mem_q_norm']), batch, seq, mem_len)
    kvc = _compress(z, w1r, pe8, w2p, gkc, batch, seq)
    kaug, vsel, kwin, vwin = _kv_pack(z, twice(p['nsa_ks_norm']), twice(p['nsa_kw_norm']), batch, seq)
    y_nsa = _nsa(z, twice(p['nsa_q_norm']), kaug, vsel, kwin, vwin, kvc, _selection_weights(seq), batch, seq)
    x2 = _merge(x1, row(p['mix_norm']), y_conv, y_nsa, y_mem, w_gm, p['w_branch'].astype(BF16),
                p['w_o'].astype(BF16))
    return _ffn(x2, row(p['ffn2_norm']), p['ffn2_w_gate'].astype(BF16), p['ffn2_w_up'].astype(BF16),
                p['ffn2_w_down'].astype(BF16))


def kernel(x, mem, ffn1_norm, ffn1_w_gate, ffn1_w_up, ffn1_w_down, mix_norm, mem_norm, w_in, conv_w, nsa_q_norm, nsa_kc_norm, nsa_ks_norm, nsa_kw_norm, cmp_pe_k, cmp_w1_k, cmp_w2_k, cmp_pe_v, cmp_w1_v, cmp_w2_v, w_mem_kv, mem_q_norm, mem_k_norm, w_branch, w_o, ffn2_norm, ffn2_w_gate, ffn2_w_up, ffn2_w_down):
    batch, seq, d = x.shape
    mem_len = mem.shape[1]
    assert seq % (SEL_TK * 4) == 0 and seq // SLC_LEN <= LANES and seq >= WIN + Q_BLOCK
    params = dict(
        ffn1_norm=ffn1_norm, ffn1_w_gate=ffn1_w_gate, ffn1_w_up=ffn1_w_up, ffn1_w_down=ffn1_w_down,
        mix_norm=mix_norm, mem_norm=mem_norm, w_in=w_in, conv_w=conv_w, nsa_q_norm=nsa_q_norm,
        nsa_kc_norm=nsa_kc_norm, nsa_ks_norm=nsa_ks_norm, nsa_kw_norm=nsa_kw_norm, cmp_pe_k=cmp_pe_k,
        cmp_w1_k=cmp_w1_k, cmp_w2_k=cmp_w2_k, cmp_pe_v=cmp_pe_v, cmp_w1_v=cmp_w1_v, cmp_w2_v=cmp_w2_v,
        w_mem_kv=w_mem_kv, mem_q_norm=mem_q_norm, mem_k_norm=mem_k_norm, w_branch=w_branch, w_o=w_o,
        ffn2_norm=ffn2_norm, ffn2_w_gate=ffn2_w_gate, ffn2_w_up=ffn2_w_up, ffn2_w_down=ffn2_w_down)
    x2d = x.reshape(batch * seq, d)
    mem2d = mem.reshape(batch * mem_len, d)
    for l in range(ffn1_norm.shape[0]):
        x2d = _layer(x2d, mem2d, batch, seq, mem_len, {k: v[l] for k, v in params.items()})
    return x2d.reshape(batch, seq, d)
```

```python
import functools
import math

import jax
import jax.numpy as jnp
import numpy as np
from jax import lax
from jax.experimental import pallas as pl
from jax.experimental.pallas import tpu as pltpu

F32 = jnp.float32
BF16 = jnp.bfloat16

EPS = 1e-6
NEG_INF = -1e30
FORCE_SCORE = 1e9

CONV_K = 3
NSA_HEAD_DIM = 64
NSA_GROUPS = 4
NSA_HPG = 4
NSA_HEADS = NSA_GROUPS * NSA_HPG
CMP_LEN = 32
CMP_STRIDE = 16
CMP_HIDDEN = 4 * NSA_HEAD_DIM
SLC_LEN = 64
SLC_TOP = 16
WIN = 512
Q_BLOCK = 128
MEM_HEADS = 4
N_BRANCH = 3

LANES = 128
SUBLANES = 8
VMEM_LIMIT_BYTES = 56 * 1024 * 1024

FFN_TM = 512
FFN_TF = 512
PROJ_TM = 1024
PROJ_TN = 768
CONV_TS = 512
MEM_TM = 512
PACK_TS = 512
SEL_TK = 512
NSA_QPB = 4
MERGE_TM = 1024
MERGE_TN = 256

Z_CONV = 0
Z_QNSA = 3072
Z_QMEM = 4096
Z_KV = 5120
Z_GATE = 6656
Z_COLS = 6912


def _params(sem):
    return pltpu.CompilerParams(dimension_semantics=sem, vmem_limit_bytes=VMEM_LIMIT_BYTES)


def _rms(x, g):
    return x * lax.rsqrt(jnp.mean(x * x, axis=-1, keepdims=True) + EPS) * g


def _sigmoid(x):
    return 1.0 / (1.0 + jnp.exp(-x))


def _dot(a, b):
    return jnp.dot(a, b, preferred_element_type=F32)


def _dot_nt(a, b):
    return lax.dot_general(a, b, (((1,), (1,)), ((), ())), preferred_element_type=F32)


def _half_norm(x, g):
    lane = lax.broadcasted_iota(jnp.int32, x.shape, 1)
    lo = lane < NSA_HEAD_DIM
    xx = x * x
    s_lo = jnp.sum(jnp.where(lo, xx, 0.0), axis=-1, keepdims=True)
    s_hi = jnp.sum(jnp.where(lo, 0.0, xx), axis=-1, keepdims=True)
    r = jnp.where(lo, lax.rsqrt(s_lo / NSA_HEAD_DIM + EPS), lax.rsqrt(s_hi / NSA_HEAD_DIM + EPS))
    return x * r * g


def _split_halves(x):
    lane = lax.broadcasted_iota(jnp.int32, x.shape, 1)
    lo = lane < NSA_HEAD_DIM
    return jnp.where(lo, x, 0.0), jnp.where(lo, pltpu.roll(x, NSA_HEAD_DIM, axis=1), 0.0)


def _ffn_kernel(x_ref, g_ref, wg_ref, wu_ref, wd_ref, o_ref, h_ref):
    j = pl.program_id(1)

    @pl.when(j == 0)
    def _():
        h_ref[...] = _rms(x_ref[...], g_ref[...]).astype(BF16)
        o_ref[...] = jnp.zeros_like(o_ref)

    h = h_ref[...]
    gate = _dot(h, wg_ref[...])
    up = _dot(h, wu_ref[...])
    act = (gate * _sigmoid(gate) * up).astype(BF16)
    o_ref[...] += _dot(act, wd_ref[...])

    @pl.when(j == pl.num_programs(1) - 1)
    def _():
        o_ref[...] = x_ref[...] + 0.5 * o_ref[...]


def _ffn(x, g, wg, wu, wd):
    n, d = x.shape
    f = wg.shape[1]
    return pl.pallas_call(
        _ffn_kernel,
        out_shape=jax.ShapeDtypeStruct((n, d), F32),
        grid=(n // FFN_TM, f // FFN_TF),
        in_specs=[
            pl.BlockSpec((FFN_TM, d), lambda i, j: (i, 0)),
            pl.BlockSpec((1, d), lambda i, j: (0, 0)),
            pl.BlockSpec((d, FFN_TF), lambda i, j: (0, j)),
            pl.BlockSpec((d, FFN_TF), lambda i, j: (0, j)),
            pl.BlockSpec((FFN_TF, d), lambda i, j: (j, 0)),
        ],
        out_specs=pl.BlockSpec((FFN_TM, d), lambda i, j: (i, 0)),
        scratch_shapes=[pltpu.VMEM((FFN_TM, d), BF16)],
        compiler_params=_params(("parallel", "arbitrary")),
        name="ffn",
    )(x, g, wg, wu, wd)


def _proj_kernel(x_ref, g_ref, w_ref, o_ref, h_ref):
    @pl.when(pl.program_id(1) == 0)
    def _():
        h_ref[...] = _rms(x_ref[...], g_ref[...]).astype(BF16)

    o_ref[...] = _dot(h_ref[...], w_ref[...])


def _in_proj(x, g, w):
    n, d = x.shape
    c = w.shape[1]
    return pl.pallas_call(
        _proj_kernel,
        out_shape=(jax.ShapeDtypeStruct((n, c), F32), jax.ShapeDtypeStruct((n, d), BF16)),
        grid=(n // PROJ_TM, c // PROJ_TN),
        in_specs=[
            pl.BlockSpec((PROJ_TM, d), lambda i, j: (i, 0)),
            pl.BlockSpec((1, d), lambda i, j: (0, 0)),
            pl.BlockSpec((d, PROJ_TN), lambda i, j: (0, j)),
        ],
        out_specs=(pl.BlockSpec((PROJ_TM, PROJ_TN), lambda i, j: (i, j)),
                   pl.BlockSpec((PROJ_TM, d), lambda i, j: (i, 0))),
        compiler_params=_params(("parallel", "arbitrary")),
        name="in_proj",
    )(x, g, w)


def _conv_kernel(b_ref, c_ref, u_ref, cp_ref, up_ref, w_ref, o_ref, z_ref):
    ts = b_ref.shape[0]
    first = pl.program_id(1) == 0
    zp = cp_ref[...] * up_ref[...]
    z_ref[pl.ds(0, SUBLANES), :] = jnp.where(first, 0.0, zp)
    z = c_ref[...] * u_ref[...]
    z_ref[pl.ds(SUBLANES, ts), :] = z
    z1 = z_ref[pl.ds(SUBLANES - 1, ts), :]
    z2 = z_ref[pl.ds(SUBLANES - 2, ts), :]
    w = w_ref[...]
    y = b_ref[...] * (w[0:1, :] * z2 + w[1:2, :] * z1 + w[2:3, :] * z)
    o_ref[...] = y.astype(o_ref.dtype)


def _conv(z, conv_w, batch, seq):
    cw = conv_w.shape[1]
    nst = seq // CONV_TS
    rows_per = CONV_TS // SUBLANES
    cb = Z_CONV // cw

    def prev_map(col):
        return lambda b, i: (jnp.maximum((b * nst + i) * rows_per - 1, 0), col)

    return pl.pallas_call(
        _conv_kernel,
        out_shape=jax.ShapeDtypeStruct((batch * seq, cw), BF16),
        grid=(batch, nst),
        in_specs=[
            pl.BlockSpec((CONV_TS, cw), lambda b, i: (b * nst + i, cb)),
            pl.BlockSpec((CONV_TS, cw), lambda b, i: (b * nst + i, cb + 1)),
            pl.BlockSpec((CONV_TS, cw), lambda b, i: (b * nst + i, cb + 2)),
            pl.BlockSpec((SUBLANES, cw), prev_map(cb + 1)),
            pl.BlockSpec((SUBLANES, cw), prev_map(cb + 2)),
            pl.BlockSpec((CONV_K, cw), lambda b, i: (0, 0)),
        ],
        out_specs=pl.BlockSpec((CONV_TS, cw), lambda b, i: (b * nst + i, 0)),
        scratch_shapes=[pltpu.VMEM((CONV_TS + SUBLANES, cw), F32)],
        compiler_params=_params(("parallel", "arbitrary")),
        name="conv",
    )(z, z, z, z, z, conv_w)


def _mem_kv_kernel(m_ref, g_ref, w_ref, gk_ref, o_ref, *, n_key_tiles):
    h = _rms(m_ref[...], g_ref[...]).astype(BF16)
    kv = _dot(h, w_ref[...])
    is_key = pl.program_id(0) < n_key_tiles
    o_ref[...] = jnp.where(is_key, _rms(kv, gk_ref[...]), kv).astype(o_ref.dtype)


def _mem_kv(mem2d, g, w, gk):
    rows, d = mem2d.shape
    hd = gk.shape[1]
    cols = w.shape[1]
    return pl.pallas_call(
        functools.partial(_mem_kv_kernel, n_key_tiles=cols // (2 * hd)),
        out_shape=jax.ShapeDtypeStruct((rows, cols), BF16),
        grid=(cols // hd,),
        in_specs=[
            pl.BlockSpec((rows, d), lambda j: (0, 0)),
            pl.BlockSpec((1, d), lambda j: (0, 0)),
            pl.BlockSpec((d, hd), lambda j: (0, j)),
            pl.BlockSpec((1, hd), lambda j: (0, 0)),
        ],
        out_specs=pl.BlockSpec((rows, hd), lambda j: (0, j)),
        compiler_params=_params(("arbitrary",)),
        name="mem_kv",
    )(mem2d, g, w, gk)


def _mem_attn_kernel(q_ref, k_ref, v_ref, gq_ref, o_ref):
    hd = gq_ref.shape[1]
    scale = hd ** -0.5
    outs = []
    for h in range(MEM_HEADS):
        sl = slice(h * hd, (h + 1) * hd)
        q = (_rms(q_ref[:, sl], gq_ref[...]) * scale).astype(BF16)
        s = _dot_nt(q, k_ref[:, sl])
        e = jnp.exp(s - jnp.max(s, axis=-1, keepdims=True))
        p = e / jnp.sum(e, axis=-1, keepdims=True)
        outs.append(_dot(p.astype(BF16), v_ref[:, sl]))
    o_ref[...] = jnp.concatenate(outs, axis=1).astype(o_ref.dtype)


def _mem_attn(z, kvm, gq, batch, seq, mem_len):
    width = MEM_HEADS * gq.shape[1]
    nt = seq // MEM_TM
    return pl.pallas_call(
        _mem_attn_kernel,
        out_shape=jax.ShapeDtypeStruct((batch * seq, width), BF16),
        grid=(batch, nt),
        in_specs=[
            pl.BlockSpec((MEM_TM, width), lambda b, i: (b * nt + i, Z_QMEM // width)),
            pl.BlockSpec((mem_len, width), lambda b, i: (b, 0)),
            pl.BlockSpec((mem_len, width), lambda b, i: (b, 1)),
            pl.BlockSpec((1, gq.shape[1]), lambda b, i: (0, 0)),
        ],
        out_specs=pl.BlockSpec((MEM_TM, width), lambda b, i: (b * nt + i, 0)),
        compiler_params=_params(("parallel", "arbitrary")),
        name="mem_attn",
    )(z, kvm, kvm, gq)


def _compress_kernel(x_ref, w1_ref, pe_ref, w2_ref, g_ref, o_ref, b_ref):
    rows = o_ref.shape[1]
    is_key = pl.program_id(1) == 0
    ab = [jnp.zeros((rows, 2 * CMP_HIDDEN), F32) for _ in range(2)]
    peb = jnp.zeros((SUBLANES, 2 * CMP_HIDDEN), F32)
    for l in range(CMP_STRIDE):
        x = x_ref[pl.ds(l, rows, stride=CMP_STRIDE), :]
        w = w1_ref[l]
        halves = _split_halves(x)
        for gg in range(2):
            ab[gg] = ab[gg] + _dot(halves[gg][:, :NSA_HEAD_DIM].astype(BF16), w)
        peb = peb + _dot(pe_ref[l], w)
    bias = peb[0:1, :CMP_HIDDEN] + peb[1:2, CMP_HIDDEN:]
    b_ref[pl.ds(rows, SUBLANES), :] = jnp.zeros((SUBLANES, CMP_HIDDEN), F32)
    for gg in range(2):
        b_ref[pl.ds(0, rows), :] = ab[gg][:, CMP_HIDDEN:]
        hid = ab[gg][:, :CMP_HIDDEN] + b_ref[pl.ds(1, rows), :] + bias
        act = (hid * _sigmoid(hid)).astype(BF16)
        out = _dot(act, w2_ref[...])
        ms = jnp.sum(out * out, axis=-1, keepdims=True) / NSA_HEAD_DIM
        normed = out * lax.rsqrt(ms + EPS) * g_ref[...]
        o_ref[gg] = jnp.where(is_key, normed, out).astype(o_ref.dtype)


def _compress(z, w1r, pe8, w2p, gkc, batch, seq):
    rows = seq // CMP_STRIDE
    cb = Z_KV // LANES
    return pl.pallas_call(
        _compress_kernel,
        out_shape=jax.ShapeDtypeStruct((batch, 2, NSA_GROUPS, rows, LANES), BF16),
        grid=(batch, 2, NSA_GROUPS // 2),
        in_specs=[
            pl.BlockSpec((seq, LANES), lambda b, s, p: (b, cb + 2 * s + p)),
            pl.BlockSpec((None, CMP_STRIDE, NSA_HEAD_DIM, 2 * CMP_HIDDEN), lambda b, s, p: (s, 0, 0, 0)),
            pl.BlockSpec((None, CMP_STRIDE, SUBLANES, NSA_HEAD_DIM), lambda b, s, p: (s, 0, 0, 0)),
            pl.BlockSpec((None, CMP_HIDDEN, LANES), lambda b, s, p: (s, 0, 0)),
            pl.BlockSpec((1, LANES), lambda b, s, p: (0, 0)),
        ],
        out_specs=pl.BlockSpec((None, None, 2, rows, LANES), lambda b, s, p: (b, s, p, 0, 0)),
        scratch_shapes=[pltpu.VMEM((rows + SUBLANES, CMP_HIDDEN), F32)],
        compiler_params=_params(("parallel", "arbitrary", "arbitrary")),
        name="compress",
    )(z, w1r, pe8, w2p, gkc)


def _kv_pack_kernel(ks_ref, vs_ref, kw_ref, vw_ref, gks_ref, gkw_ref, ka_ref, vs_o, kw_o, vw_o):
    ts = ks_ref.shape[0]
    n_tiles = pl.num_programs(2) - 1
    i = pl.program_id(2)
    is_pad = i == n_tiles
    ksn = _split_halves(_half_norm(ks_ref[...], gks_ref[...]))
    kwn = _split_halves(_half_norm(kw_ref[...], gkw_ref[...]))
    vsh = _split_halves(vs_ref[...])
    vwh = _split_halves(vw_ref[...])
    pos = jnp.minimum(i, n_tiles - 1) * ts + lax.broadcasted_iota(jnp.int32, (ts, LANES), 0)
    blk = lax.broadcasted_iota(jnp.int32, (ts, LANES), 1)
    onehot = jnp.where(pos // SLC_LEN == blk, 1.0, 0.0)
    ones_col = jnp.where(blk == NSA_HEAD_DIM, 1.0, 0.0)
    for gg in range(2):
        ka_ref[gg] = jnp.concatenate([ksn[gg], onehot], axis=1).astype(ka_ref.dtype)
        vs_o[gg] = (vsh[gg] + ones_col).astype(vs_o.dtype)
        kw_o[gg] = jnp.where(is_pad, ones_col, kwn[gg]).astype(kw_o.dtype)
        vw_o[gg] = jnp.where(is_pad, 0.0, vwh[gg] + ones_col).astype(vw_o.dtype)


def _kv_pack(z, gks, gkw, batch, seq):
    assert PACK_TS == WIN
    nst = seq // PACK_TS
    cb = Z_KV // LANES + 4

    def zspec(k):
        return pl.BlockSpec((PACK_TS, LANES), lambda b, p, i: (b * nst + jnp.minimum(i, nst - 1), cb + 2 * k + p))

    def ospec(w):
        return pl.BlockSpec((None, 2, PACK_TS, w), lambda b, p, i: (b, p, jnp.minimum(i, nst - 1), 0))

    def wspec(w):
        return pl.BlockSpec((None, 2, PACK_TS, w), lambda b, p, i: (b, p, (i + 1) % (nst + 1), 0))

    def oshape(rows, w):
        return jax.ShapeDtypeStruct((batch, NSA_GROUPS, rows, w), BF16)

    gspec = pl.BlockSpec((1, LANES), lambda b, p, i: (0, 0))
    return pl.pallas_call(
        _kv_pack_kernel,
        out_shape=(oshape(seq, 2 * LANES), oshape(seq, LANES), oshape(seq + WIN, LANES), oshape(seq + WIN, LANES)),
        grid=(batch, NSA_GROUPS // 2, nst + 1),
        in_specs=[zspec(0), zspec(1), zspec(2), zspec(3), gspec, gspec],
        out_specs=(ospec(2 * LANES), ospec(LANES), wspec(LANES), wspec(LANES)),
        compiler_params=_params(("parallel", "arbitrary", "arbitrary")),
        name="kv_pack",
    )(z, z, z, z, gks, gkw)


def _count_outranking(score_ref, cnt_ref, n_live):
    nb, nq = score_ref.shape
    n_chunks = nb // SUBLANES
    cnt_ref[...] = jnp.zeros(cnt_ref.shape, cnt_ref.dtype)
    for ci in range(n_chunks):
        @pl.when(ci < n_live)
        def _():
            sub = lax.broadcasted_iota(jnp.int32, (SUBLANES, nq), 0)
            chunks = [score_ref[SUBLANES * v:SUBLANES * (v + 1), :] for v in range(n_chunks)]
            cnt = [cnt_ref[SUBLANES * v:SUBLANES * (v + 1), :] for v in range(n_chunks)]
            for r in range(SUBLANES):
                row = jnp.broadcast_to(chunks[ci][r:r + 1, :], (SUBLANES, nq))
                for v, c in enumerate(chunks):
                    if v < ci:
                        beats = row > c
                    elif v > ci:
                        beats = row >= c
                    else:
                        beats = ((sub < r) & (row > c)) | ((sub > r) & (row >= c))
                    cnt[v] = cnt[v] + jnp.where(beats, 1, 0)
            for v in range(n_chunks):
                cnt_ref[SUBLANES * v:SUBLANES * (v + 1), :] = cnt[v]


def _nsa_kernel(q_ref, gate_ref, gq_ref, ka_ref, vs_ref, kw_ref, vw_ref, kc_ref, vc_ref, map_ref, band_ref,
                o_ref, m_ref, acc_ref, sa_ref, sb_ref, mta_ref, mtb_ref, qa_ref, score_ref, cnt_ref, oc_ref, ow_ref):
    g = pl.program_id(1)
    t0 = pl.program_id(2) * (NSA_QPB * Q_BLOCK)
    sub = NSA_HPG * Q_BLOCK
    rows = NSA_QPB * sub
    n_q = NSA_QPB * Q_BLOCK
    n_cmp = kc_ref.shape[0]
    scale = NSA_HEAD_DIM ** -0.5 * math.log2(math.e)
    lane = lax.broadcasted_iota(jnp.int32, (1, LANES), 1)

    def block_rows(j):
        return slice(j * sub, (j + 1) * sub)

    heads = []
    for j in range(NSA_QPB):
        for pair in range(NSA_HPG // 2):
            qn = _half_norm(q_ref[j * Q_BLOCK:(j + 1) * Q_BLOCK, pair * LANES:(pair + 1) * LANES], gq_ref[...])
            heads.extend(_split_halves(qn * scale))
    q = jnp.where(lane == NSA_HEAD_DIM, NEG_INF, jnp.concatenate(heads, axis=0)).astype(BF16)
    row = lax.broadcasted_iota(jnp.int32, (rows, 1), 0)
    t = t0 + (row // sub) * Q_BLOCK + row % Q_BLOCK

    span = WIN + Q_BLOCK
    band = jnp.concatenate([band_ref[...]] * NSA_HPG, axis=0)
    for j in range(NSA_QPB):
        rs = block_rows(j)
        start = pl.multiple_of(t0 + j * Q_BLOCK, Q_BLOCK)
        s = _dot_nt(q[rs], kw_ref[pl.ds(start, span), :]) + band
        e = jnp.exp2(s - jnp.max(s, axis=-1, keepdims=True))
        ow_ref[rs, :] = _dot(e.astype(BF16), vw_ref[pl.ds(start, span), :])

    s = _dot_nt(q, kc_ref[...])
    cmp_end = lax.broadcasted_iota(jnp.int32, (1, n_cmp), 1) * CMP_STRIDE + (CMP_LEN - 1)
    valid = cmp_end <= t
    s = jnp.where(valid, s, NEG_INF)
    e = jnp.exp2(s - jnp.maximum(jnp.max(s, axis=-1, keepdims=True), 0.1 * NEG_INF))
    p = e / jnp.maximum(jnp.sum(e, axis=-1, keepdims=True), 1e-30)
    oc_ref[...] = _dot(p.astype(BF16), vc_ref[...])
    imps = []
    for j in range(NSA_QPB):
        imp = p[j * sub:j * sub + Q_BLOCK]
        for h in range(1, NSA_HPG):
            imp = imp + p[j * sub + h * Q_BLOCK:j * sub + (h + 1) * Q_BLOCK]
        imps.append(imp)
    imp = jnp.concatenate(imps, axis=0)
    hi = imp.astype(BF16)
    r1 = imp - hi.astype(F32)
    mid = r1.astype(BF16)
    lo = (r1 - mid.astype(F32)).astype(BF16)
    wmap = map_ref[...]
    imp_s = _dot(hi, wmap) + _dot(mid, wmap) + _dot(lo, wmap)
    tq = t0 + lax.broadcasted_iota(jnp.int32, (n_q, 1), 0)
    cur = tq // SLC_LEN
    blk = lax.broadcasted_iota(jnp.int32, (1, LANES), 1)
    forced = (blk == 0) | (blk == cur) | (blk == cur - 1)
    score = jnp.where(forced, FORCE_SCORE, jnp.where(blk <= cur, imp_s, NEG_INF))
    score_ref[...] = score.T
    _count_outranking(score_ref, cnt_ref, (t0 + n_q - 1) // SLC_LEN // SUBLANES + 1)
    chosen = jnp.where(cnt_ref[...] < SLC_TOP, 0.0, NEG_INF).T
    bias = jnp.where(blk <= cur, chosen, NEG_INF).astype(BF16)
    bias_rows = [bias[j * Q_BLOCK:(j + 1) * Q_BLOCK] for j in range(NSA_QPB) for _ in range(NSA_HPG)]
    qa_ref[...] = jnp.concatenate([q, jnp.concatenate(bias_rows, axis=0)], axis=1)

    n_cols = SEL_TK // LANES
    m_ref[...] = jnp.full(m_ref.shape, NEG_INF, F32)
    acc_ref[...] = jnp.zeros(acc_ref.shape, F32)

    def tile_max(s_ref):
        mx = s_ref[0]
        for c in range(1, n_cols):
            mx = jnp.maximum(mx, s_ref[c])
        return jnp.broadcast_to(jnp.max(mx, axis=-1, keepdims=True), mx.shape)

    def scores(kt, s_ref, mt_ref):
        k0 = pl.multiple_of(kt * SEL_TK, SEL_TK)
        s = _dot_nt(qa_ref[...], ka_ref[pl.ds(k0, SEL_TK), :])
        for c in range(n_cols):
            s_ref[c] = s[:, c * LANES:(c + 1) * LANES]
        mt_ref[...] = tile_max(s_ref)

    def update(kt, s_ref, mt_ref):
        k0 = pl.multiple_of(kt * SEL_TK, SEL_TK)
        m_prev = m_ref[...]
        m_new = jnp.maximum(m_prev, mt_ref[...])
        alpha = jnp.exp2(m_prev - m_new)
        p = jnp.concatenate([jnp.exp2(s_ref[c] - m_new) for c in range(n_cols)], axis=1)
        acc_ref[...] = alpha * acc_ref[...] + _dot(p.astype(BF16), vs_ref[pl.ds(k0, SEL_TK), :])
        m_ref[...] = m_new

    n_full = t0 // SEL_TK
    odd = n_full % 2

    @pl.when(odd == 0)
    def _():
        scores(0, sa_ref, mta_ref)

    @pl.when(odd == 1)
    def _():
        scores(0, sb_ref, mtb_ref)
        scores(1, sa_ref, mta_ref)
        update(0, sb_ref, mtb_ref)

    def pair(j, carry):
        kt = odd + 2 * j
        scores(kt + 1, sb_ref, mtb_ref)
        update(kt, sa_ref, mta_ref)
        scores(kt + 2, sa_ref, mta_ref)
        update(kt + 1, sb_ref, mtb_ref)
        return carry

    lax.fori_loop(0, n_full // 2, pair, 0)
    for j in range(NSA_QPB):
        rs = block_rows(j)
        tj = t0 + j * Q_BLOCK
        dcol = (tj % SEL_TK) // LANES
        kpos = tj + lax.broadcasted_iota(jnp.int32, (1, LANES), 1)
        sa_ref[dcol, rs, :] = jnp.where(kpos <= t[rs], sa_ref[dcol, rs, :], NEG_INF)
    mta_ref[...] = tile_max(sa_ref)
    update(n_full, sa_ref, mta_ref)

    def normalised(acc):
        denom = jnp.sum(jnp.where(lane == NSA_HEAD_DIM, acc, 0.0), axis=-1, keepdims=True)
        return jnp.where(lane < NSA_HEAD_DIM, acc / denom, 0.0)

    for j in range(NSA_QPB):
        qs = slice(j * Q_BLOCK, (j + 1) * Q_BLOCK)
        gates = _sigmoid(gate_ref[qs, :])
        outs = []
        for h in range(NSA_HPG):
            head = g * NSA_HPG + h
            rs = slice(j * sub + h * Q_BLOCK, j * sub + (h + 1) * Q_BLOCK)
            o = jnp.zeros((Q_BLOCK, LANES), F32)
            for r, branch in enumerate((oc_ref[rs, :], normalised(acc_ref[rs, :]), normalised(ow_ref[rs, :]))):
                gcol = jnp.sum(jnp.where(lane == r * NSA_HEADS + head, gates, 0.0), axis=-1, keepdims=True)
                o = o + gcol * branch
            outs.append(o)
        pairs = [outs[2 * k] + pltpu.roll(outs[2 * k + 1], NSA_HEAD_DIM, axis=1) for k in range(NSA_HPG // 2)]
        o_ref[qs, :] = jnp.concatenate(pairs, axis=1).astype(o_ref.dtype)


def _nsa(z, gq, kaug, vsel, kwin, vwin, kvc, wmap, batch, seq):
    n_q = NSA_QPB * Q_BLOCK
    nq = seq // n_q
    gw = NSA_HPG * NSA_HEAD_DIM
    rows = NSA_HPG * n_q
    n_cmp = seq // CMP_STRIDE

    def kvspec(w, pad=0):
        return pl.BlockSpec((None, None, seq + pad, w), lambda b, g, i: (b, g, 0, 0))

    def cspec(s):
        return pl.BlockSpec((None, None, None, n_cmp, LANES), lambda b, g, i: (b, s, g, 0, 0))

    r_idx = np.arange(Q_BLOCK)[:, None]
    c_idx = np.arange(WIN + Q_BLOCK)[None, :]
    band = jnp.asarray(np.where((c_idx > r_idx) & (c_idx <= r_idx + WIN), 0.0, NEG_INF), F32)

    return pl.pallas_call(
        _nsa_kernel,
        out_shape=jax.ShapeDtypeStruct((batch * seq, NSA_HEADS * NSA_HEAD_DIM), BF16),
        grid=(batch, NSA_GROUPS, nq),
        in_specs=[
            pl.BlockSpec((n_q, gw), lambda b, g, i: (b * nq + i, Z_QNSA // gw + g)),
            pl.BlockSpec((n_q, LANES), lambda b, g, i: (b * nq + i, Z_GATE // LANES)),
            pl.BlockSpec((1, LANES), lambda b, g, i: (0, 0)),
            kvspec(2 * LANES), kvspec(LANES), kvspec(LANES, WIN), kvspec(LANES, WIN),
            cspec(0), cspec(1),
            pl.BlockSpec((n_cmp, LANES), lambda b, g, i: (0, 0)),
            pl.BlockSpec((Q_BLOCK, WIN + Q_BLOCK), lambda b, g, i: (0, 0)),
        ],
        out_specs=pl.BlockSpec((n_q, gw), lambda b, g, i: (b * nq + i, g)),
        scratch_shapes=[
            pltpu.VMEM((rows, LANES), F32), pltpu.VMEM((rows, LANES), F32),
            pltpu.VMEM((SEL_TK // LANES, rows, LANES), F32), pltpu.VMEM((SEL_TK // LANES, rows, LANES), F32),
            pltpu.VMEM((rows, LANES), F32), pltpu.VMEM((rows, LANES), F32),
            pltpu.VMEM((rows, 2 * LANES), BF16),
            pltpu.VMEM((LANES, n_q), F32), pltpu.VMEM((LANES, n_q), jnp.int32),
            pltpu.VMEM((rows, LANES), F32), pltpu.VMEM((rows, LANES), F32)],
        compiler_params=_params(("parallel", "parallel", "arbitrary")),
        name="nsa",
    )(z, z, gq, kaug, vsel, kwin, vwin, kvc, kvc, wmap, band)


def _selection_weights(seq):
    n_c = (seq - CMP_LEN) // CMP_STRIDE + 1
    n_s = seq // SLC_LEN
    ratio = SLC_LEN // CMP_STRIDE
    w = np.zeros((seq // CMP_STRIDE, LANES), np.float32)
    for j in range(n_s):
        for off in range(-(CMP_LEN // CMP_STRIDE - 1), ratio):
            ci = ratio * j + off
            if 0 <= ci < n_c:
                cs = ci * CMP_STRIDE
                ov = min(cs + CMP_LEN, (j + 1) * SLC_LEN) - max(cs, j * SLC_LEN)
                w[ci, j] = max(ov, 0) / CMP_LEN
    return jnp.asarray(w, BF16)


def _merge_kernel(h_ref, xo_ref, yc_ref, yn_ref, ym_ref, wgc_ref, wgn_ref, wgm_ref, wb_ref, wo_ref, o_ref, mg_ref, *,
                  n_chunks):
    j = pl.program_id(1)

    @pl.when(j < n_chunks)
    def _():
        h = h_ref[...]
        merged = jnp.zeros((h.shape[0], wb_ref.shape[2]), F32)
        for n, (y_ref, wg_ref) in enumerate(((yc_ref, wgc_ref), (yn_ref, wgn_ref), (ym_ref, wgm_ref))):
            merged = merged + _sigmoid(_dot(h, wg_ref[...])) * _dot(y_ref[...], wb_ref[n])
        mg_ref[j] = merged.astype(BF16)

    @pl.when(j >= n_chunks)
    def _():
        acc = jnp.zeros(o_ref.shape, F32)
        for c in range(n_chunks):
            acc = acc + _dot(mg_ref[c], wo_ref[c])
        o_ref[...] = xo_ref[...] + acc


def _merge(x, h, yc, yn, ym, wg, wb, wo):
    n, d = x.shape
    bw = yc.shape[1]
    nc = d // MERGE_TN
    wo_c = wo.reshape(nc, MERGE_TN, d)

    def gspec(branch):
        return pl.BlockSpec((d, MERGE_TN), lambda i, j: (0, branch * nc + jnp.minimum(j, nc - 1)))

    return pl.pallas_call(
        functools.partial(_merge_kernel, n_chunks=nc),
        out_shape=jax.ShapeDtypeStruct((n, d), F32),
        grid=(n // MERGE_TM, 2 * nc),
        in_specs=[
            pl.BlockSpec((MERGE_TM, d), lambda i, j: (i, 0)),
            pl.BlockSpec((MERGE_TM, MERGE_TN), lambda i, j: (i, jnp.maximum(j - nc, 0))),
            pl.BlockSpec((MERGE_TM, bw), lambda i, j: (i, 0)),
            pl.BlockSpec((MERGE_TM, bw), lambda i, j: (i, 0)),
            pl.BlockSpec((MERGE_TM, bw), lambda i, j: (i, 0)),
            gspec(0), gspec(1), gspec(2),
            pl.BlockSpec((N_BRANCH, bw, MERGE_TN), lambda i, j: (0, 0, jnp.minimum(j, nc - 1))),
            pl.BlockSpec((nc, MERGE_TN, MERGE_TN), lambda i, j: (0, 0, jnp.maximum(j - nc, 0))),
        ],
        out_specs=pl.BlockSpec((MERGE_TM, MERGE_TN), lambda i, j: (i, jnp.maximum(j - nc, 0))),
        scratch_shapes=[pltpu.VMEM((nc, MERGE_TM, MERGE_TN), BF16)],
        compiler_params=_params(("parallel", "arbitrary")),
        name="merge",
    )(h, x, yc, yn, ym, wg, wg, wg, wb, wo_c)


def _layer(x2d, mem2d, batch, seq, mem_len, p):
    d = x2d.shape[1]
    bw = d // 2
    kvw = NSA_GROUPS * NSA_HEAD_DIM

    w_in = p['w_in'].astype(BF16)
    o_q = 3 * bw
    o_kv = o_q + NSA_HEADS * NSA_HEAD_DIM
    o_g = o_kv + 6 * kvw
    o_qm = o_g + 3 * NSA_HEADS
    o_gm = o_qm + bw
    gate_cols = w_in[:, o_g:o_qm].reshape(d, NSA_HEADS, 3).transpose(0, 2, 1).reshape(d, 3 * NSA_HEADS)
    w_proj = jnp.concatenate([
        w_in[:, :o_q], w_in[:, o_q:o_kv], w_in[:, o_qm:o_gm], w_in[:, o_kv:o_g], gate_cols,
        jnp.zeros((d, Z_COLS - Z_GATE - 3 * NSA_HEADS), BF16)], axis=1)
    w_gm = w_in[:, o_gm:]

    def cmp_weights(w1, w2, pe):
        w1l = w1.reshape(CMP_LEN, NSA_HEAD_DIM, CMP_HIDDEN)
        w1r = jnp.concatenate([w1l[:CMP_STRIDE], w1l[CMP_STRIDE:]], axis=2)
        pe8 = jnp.zeros((CMP_STRIDE, SUBLANES, NSA_HEAD_DIM), F32)
        pe8 = pe8.at[:, 0].set(pe[:CMP_STRIDE]).at[:, 1].set(pe[CMP_STRIDE:])
        w2p = jnp.pad(w2, ((0, 0), (0, LANES - NSA_HEAD_DIM)))
        return w1r.astype(BF16), pe8.astype(BF16), w2p.astype(BF16)

    ck = cmp_weights(p['cmp_w1_k'], p['cmp_w2_k'], p['cmp_pe_k'])
    cv = cmp_weights(p['cmp_w1_v'], p['cmp_w2_v'], p['cmp_pe_v'])
    w1r, pe8, w2p = (jnp.stack([a, b]) for a, b in zip(ck, cv))

    def row(v):
        return v.reshape(1, -1)

    def twice(v):
        return jnp.concatenate([v, v]).reshape(1, LANES)

    gkc = jnp.pad(p['nsa_kc_norm'], (0, LANES - NSA_HEAD_DIM)).reshape(1, LANES)

    x1 = _ffn(x2d, row(p['ffn1_norm']), p['ffn1_w_gate'].astype(BF16), p['ffn1_w_up'].astype(BF16),
              p['ffn1_w_down'].astype(BF16))
    z, h_mix = _in_proj(x1, row(p['mix_norm']), w_proj)
    y_conv = _conv(z, p['conv_w'], batch, seq)
    kvm = _mem_kv(mem2d, row(p['mem_norm']), p['w_mem_kv'].astype(BF16), row(p['mem_k_norm']))
    y_mem = _mem_attn(z, kvm, row(p['mem_q_norm']), batch, seq, mem_len)
    kvc = _compress(z, w1r, pe8, w2p, gkc, batch, seq)
    kaug, vsel, kwin, vwin = _kv_pack(z, twice(p['nsa_ks_norm']), twice(p['nsa_kw_norm']), batch, seq)
    y_nsa = _nsa(z, twice(p['nsa_q_norm']), kaug, vsel, kwin, vwin, kvc, _selection_weights(seq), batch, seq)
    x2 = _merge(x1, h_mix, y_conv, y_nsa, y_mem, w_gm, p['w_branch'].astype(BF16),
                p['w_o'].astype(BF16))
    return _ffn(x2, row(p['ffn2_norm']), p['ffn2_w_gate'].astype(BF16), p['ffn2_w_up'].astype(BF16),
                p['ffn2_w_down'].astype(BF16))


def kernel(x, mem, ffn1_norm, ffn1_w_gate, ffn1_w_up, ffn1_w_down, mix_norm, mem_norm, w_in, conv_w, nsa_q_norm, nsa_kc_norm, nsa_ks_norm, nsa_kw_norm, cmp_pe_k, cmp_w1_k, cmp_w2_k, cmp_pe_v, cmp_w1_v, cmp_w2_v, w_mem_kv, mem_q_norm, mem_k_norm, w_branch, w_o, ffn2_norm, ffn2_w_gate, ffn2_w_up, ffn2_w_down):
    batch, seq, d = x.shape
    mem_len = mem.shape[1]
    assert seq % (SEL_TK * 4) == 0 and seq // SLC_LEN <= LANES and seq >= WIN + Q_BLOCK
    params = dict(
        ffn1_norm=ffn1_norm, ffn1_w_gate=ffn1_w_gate, ffn1_w_up=ffn1_w_up, ffn1_w_down=ffn1_w_down,
        mix_norm=mix_norm, mem_norm=mem_norm, w_in=w_in, conv_w=conv_w, nsa_q_norm=nsa_q_norm,
        nsa_kc_norm=nsa_kc_norm, nsa_ks_norm=nsa_ks_norm, nsa_kw_norm=nsa_kw_norm, cmp_pe_k=cmp_pe_k,
        cmp_w1_k=cmp_w1_k, cmp_w2_k=cmp_w2_k, cmp_pe_v=cmp_pe_v, cmp_w1_v=cmp_w1_v, cmp_w2_v=cmp_w2_v,
        w_mem_kv=w_mem_kv, mem_q_norm=mem_q_norm, mem_k_norm=mem_k_norm, w_branch=w_branch, w_o=w_o,
        ffn2_norm=ffn2_norm, ffn2_w_gate=ffn2_w_gate, ffn2_w_up=ffn2_w_up, ffn2_w_down=ffn2_w_down)
    x2d = x.reshape(batch * seq, d)
    mem2d = mem.reshape(batch * mem_len, d)
    for l in range(ffn1_norm.shape[0]):
        x2d = _layer(x2d, mem2d, batch, seq, mem_len, {k: v[l] for k, v in params.items()})
    return x2d.reshape(batch, seq, d)
```

```python
import functools
import math

import jax
import jax.numpy as jnp
import numpy as np
from jax import lax
from jax.experimental import pallas as pl
from jax.experimental.pallas import tpu as pltpu

F32 = jnp.float32
BF16 = jnp.bfloat16

EPS = 1e-6
NEG_INF = -1e30
FORCE_SCORE = 1e9

CONV_K = 3
NSA_HEAD_DIM = 64
NSA_GROUPS = 4
NSA_HPG = 4
NSA_HEADS = NSA_GROUPS * NSA_HPG
CMP_LEN = 32
CMP_STRIDE = 16
CMP_HIDDEN = 4 * NSA_HEAD_DIM
SLC_LEN = 64
SLC_TOP = 16
WIN = 512
Q_BLOCK = 128
MEM_HEADS = 4
N_BRANCH = 3

LANES = 128
SUBLANES = 8
VMEM_LIMIT_BYTES = 56 * 1024 * 1024

FFN_TM = 512
FFN_TF = 512
PROJ_TM = 1024
PROJ_TN = 768
CONV_TS = 1024
MEM_TM = 1024
PACK_TS = 512
SEL_TK = 512
NSA_QPB = 4
MERGE_TM = 1024
MERGE_TN = 256

Z_CONV = 0
Z_QNSA = 3072
Z_QMEM = 4096
Z_KV = 5120
Z_GATE = 6656
Z_COLS = 6912


def _params(sem):
    return pltpu.CompilerParams(dimension_semantics=sem, vmem_limit_bytes=VMEM_LIMIT_BYTES)


def _rms(x, g):
    return x * lax.rsqrt(jnp.mean(x * x, axis=-1, keepdims=True) + EPS) * g


def _sigmoid(x):
    return 1.0 / (1.0 + jnp.exp(-x))


def _dot(a, b):
    return jnp.dot(a, b, preferred_element_type=F32)


def _dot_nt(a, b):
    return lax.dot_general(a, b, (((1,), (1,)), ((), ())), preferred_element_type=F32)


def _half_norm(x, g):
    lane = lax.broadcasted_iota(jnp.int32, x.shape, 1)
    lo = lane < NSA_HEAD_DIM
    xx = x * x
    s_lo = jnp.sum(jnp.where(lo, xx, 0.0), axis=-1, keepdims=True)
    s_hi = jnp.sum(jnp.where(lo, 0.0, xx), axis=-1, keepdims=True)
    r = jnp.where(lo, lax.rsqrt(s_lo / NSA_HEAD_DIM + EPS), lax.rsqrt(s_hi / NSA_HEAD_DIM + EPS))
    return x * r * g


def _split_halves(x):
    lane = lax.broadcasted_iota(jnp.int32, x.shape, 1)
    lo = lane < NSA_HEAD_DIM
    return jnp.where(lo, x, 0.0), jnp.where(lo, pltpu.roll(x, NSA_HEAD_DIM, axis=1), 0.0)


def _ffn_kernel(x_ref, g_ref, wg_ref, wu_ref, wd_ref, o_ref, h_ref):
    j = pl.program_id(1)

    @pl.when(j == 0)
    def _():
        h_ref[...] = _rms(x_ref[...], g_ref[...]).astype(BF16)
        o_ref[...] = jnp.zeros_like(o_ref)

    h = h_ref[...]
    gate = _dot(h, wg_ref[...])
    up = _dot(h, wu_ref[...])
    act = (gate * _sigmoid(gate) * up).astype(BF16)
    o_ref[...] += _dot(act, wd_ref[...])

    @pl.when(j == pl.num_programs(1) - 1)
    def _():
        o_ref[...] = x_ref[...] + 0.5 * o_ref[...]


def _ffn(x, g, wg, wu, wd):
    n, d = x.shape
    f = wg.shape[1]
    return pl.pallas_call(
        _ffn_kernel,
        out_shape=jax.ShapeDtypeStruct((n, d), F32),
        grid=(n // FFN_TM, f // FFN_TF),
        in_specs=[
            pl.BlockSpec((FFN_TM, d), lambda i, j: (i, 0)),
            pl.BlockSpec((1, d), lambda i, j: (0, 0)),
            pl.BlockSpec((d, FFN_TF), lambda i, j: (0, j)),
            pl.BlockSpec((d, FFN_TF), lambda i, j: (0, j)),
            pl.BlockSpec((FFN_TF, d), lambda i, j: (j, 0)),
        ],
        out_specs=pl.BlockSpec((FFN_TM, d), lambda i, j: (i, 0)),
        scratch_shapes=[pltpu.VMEM((FFN_TM, d), BF16)],
        compiler_params=_params(("parallel", "arbitrary")),
        name="ffn",
    )(x, g, wg, wu, wd)


def _proj_kernel(x_ref, g_ref, w_ref, o_ref, h_ref):
    @pl.when(pl.program_id(1) == 0)
    def _():
        h_ref[...] = _rms(x_ref[...], g_ref[...]).astype(BF16)

    o_ref[...] = _dot(h_ref[...], w_ref[...])


def _in_proj(x, g, w):
    n, d = x.shape
    c = w.shape[1]
    return pl.pallas_call(
        _proj_kernel,
        out_shape=(jax.ShapeDtypeStruct((n, c), F32), jax.ShapeDtypeStruct((n, d), BF16)),
        grid=(n // PROJ_TM, c // PROJ_TN),
        in_specs=[
            pl.BlockSpec((PROJ_TM, d), lambda i, j: (i, 0)),
            pl.BlockSpec((1, d), lambda i, j: (0, 0)),
            pl.BlockSpec((d, PROJ_TN), lambda i, j: (0, j)),
        ],
        out_specs=(pl.BlockSpec((PROJ_TM, PROJ_TN), lambda i, j: (i, j)),
                   pl.BlockSpec((PROJ_TM, d), lambda i, j: (i, 0))),
        compiler_params=_params(("parallel", "arbitrary")),
        name="in_proj",
    )(x, g, w)


def _conv_kernel(b_ref, c_ref, u_ref, cp_ref, up_ref, w_ref, o_ref, z_ref):
    ts = b_ref.shape[0]
    first = pl.program_id(1) == 0
    zp = cp_ref[...] * up_ref[...]
    z_ref[pl.ds(0, SUBLANES), :] = jnp.where(first, 0.0, zp)
    z = c_ref[...] * u_ref[...]
    z_ref[pl.ds(SUBLANES, ts), :] = z
    z1 = z_ref[pl.ds(SUBLANES - 1, ts), :]
    z2 = z_ref[pl.ds(SUBLANES - 2, ts), :]
    w = w_ref[...]
    y = b_ref[...] * (w[0:1, :] * z2 + w[1:2, :] * z1 + w[2:3, :] * z)
    o_ref[...] = y.astype(o_ref.dtype)


def _conv(z, conv_w, batch, seq):
    cw = conv_w.shape[1]
    nst = seq // CONV_TS
    rows_per = CONV_TS // SUBLANES
    cb = Z_CONV // cw

    def prev_map(col):
        return lambda b, i: (jnp.maximum((b * nst + i) * rows_per - 1, 0), col)

    return pl.pallas_call(
        _conv_kernel,
        out_shape=jax.ShapeDtypeStruct((batch * seq, cw), BF16),
        grid=(batch, nst),
        in_specs=[
            pl.BlockSpec((CONV_TS, cw), lambda b, i: (b * nst + i, cb)),
            pl.BlockSpec((CONV_TS, cw), lambda b, i: (b * nst + i, cb + 1)),
            pl.BlockSpec((CONV_TS, cw), lambda b, i: (b * nst + i, cb + 2)),
            pl.BlockSpec((SUBLANES, cw), prev_map(cb + 1)),
            pl.BlockSpec((SUBLANES, cw), prev_map(cb + 2)),
            pl.BlockSpec((CONV_K, cw), lambda b, i: (0, 0)),
        ],
        out_specs=pl.BlockSpec((CONV_TS, cw), lambda b, i: (b * nst + i, 0)),
        scratch_shapes=[pltpu.VMEM((CONV_TS + SUBLANES, cw), F32)],
        compiler_params=_params(("parallel", "arbitrary")),
        name="conv",
    )(z, z, z, z, z, conv_w)


def _mem_kv_kernel(m_ref, g_ref, w_ref, gk_ref, o_ref, *, n_key_tiles):
    h = _rms(m_ref[...], g_ref[...]).astype(BF16)
    kv = _dot(h, w_ref[...])
    is_key = pl.program_id(0) < n_key_tiles
    o_ref[...] = jnp.where(is_key, _rms(kv, gk_ref[...]), kv).astype(o_ref.dtype)


def _mem_kv(mem2d, g, w, gk):
    rows, d = mem2d.shape
    hd = gk.shape[1]
    cols = w.shape[1]
    return pl.pallas_call(
        functools.partial(_mem_kv_kernel, n_key_tiles=cols // (2 * hd)),
        out_shape=jax.ShapeDtypeStruct((rows, cols), BF16),
        grid=(cols // hd,),
        in_specs=[
            pl.BlockSpec((rows, d), lambda j: (0, 0)),
            pl.BlockSpec((1, d), lambda j: (0, 0)),
            pl.BlockSpec((d, hd), lambda j: (0, j)),
            pl.BlockSpec((1, hd), lambda j: (0, 0)),
        ],
        out_specs=pl.BlockSpec((rows, hd), lambda j: (0, j)),
        compiler_params=_params(("arbitrary",)),
        name="mem_kv",
    )(mem2d, g, w, gk)


def _mem_attn_kernel(q_ref, k_ref, v_ref, gq_ref, o_ref):
    hd = gq_ref.shape[1]
    scale = hd ** -0.5
    outs = []
    for h in range(MEM_HEADS):
        sl = slice(h * hd, (h + 1) * hd)
        q = (_rms(q_ref[:, sl], gq_ref[...]) * scale).astype(BF16)
        s = _dot_nt(q, k_ref[:, sl])
        e = jnp.exp(s - jnp.max(s, axis=-1, keepdims=True))
        p = e / jnp.sum(e, axis=-1, keepdims=True)
        outs.append(_dot(p.astype(BF16), v_ref[:, sl]))
    o_ref[...] = jnp.concatenate(outs, axis=1).astype(o_ref.dtype)


def _mem_attn(z, kvm, gq, batch, seq, mem_len):
    width = MEM_HEADS * gq.shape[1]
    nt = seq // MEM_TM
    return pl.pallas_call(
        _mem_attn_kernel,
        out_shape=jax.ShapeDtypeStruct((batch * seq, width), BF16),
        grid=(batch, nt),
        in_specs=[
            pl.BlockSpec((MEM_TM, width), lambda b, i: (b * nt + i, Z_QMEM // width)),
            pl.BlockSpec((mem_len, width), lambda b, i: (b, 0)),
            pl.BlockSpec((mem_len, width), lambda b, i: (b, 1)),
            pl.BlockSpec((1, gq.shape[1]), lambda b, i: (0, 0)),
        ],
        out_specs=pl.BlockSpec((MEM_TM, width), lambda b, i: (b * nt + i, 0)),
        compiler_params=_params(("parallel", "arbitrary")),
        name="mem_attn",
    )(z, kvm, kvm, gq)


def _compress_kernel(x_ref, w1_ref, pe_ref, w2_ref, g_ref, o_ref, b_ref):
    rows = o_ref.shape[1]
    is_key = pl.program_id(1) == 0
    ab = [jnp.zeros((rows, 2 * CMP_HIDDEN), F32) for _ in range(2)]
    peb = jnp.zeros((SUBLANES, 2 * CMP_HIDDEN), F32)
    for l in range(CMP_STRIDE):
        x = x_ref[pl.ds(l, rows, stride=CMP_STRIDE), :]
        w = w1_ref[l]
        halves = _split_halves(x)
        for gg in range(2):
            ab[gg] = ab[gg] + _dot(halves[gg][:, :NSA_HEAD_DIM].astype(BF16), w)
        peb = peb + _dot(pe_ref[l], w)
    bias = peb[0:1, :CMP_HIDDEN] + peb[1:2, CMP_HIDDEN:]
    b_ref[pl.ds(rows, SUBLANES), :] = jnp.zeros((SUBLANES, CMP_HIDDEN), F32)
    for gg in range(2):
        b_ref[pl.ds(0, rows), :] = ab[gg][:, CMP_HIDDEN:]
        hid = ab[gg][:, :CMP_HIDDEN] + b_ref[pl.ds(1, rows), :] + bias
        act = (hid * _sigmoid(hid)).astype(BF16)
        out = _dot(act, w2_ref[...])
        ms = jnp.sum(out * out, axis=-1, keepdims=True) / NSA_HEAD_DIM
        normed = out * lax.rsqrt(ms + EPS) * g_ref[...]
        o_ref[gg] = jnp.where(is_key, normed, out).astype(o_ref.dtype)


def _compress(z, w1r, pe8, w2p, gkc, batch, seq):
    rows = seq // CMP_STRIDE
    cb = Z_KV // LANES
    return pl.pallas_call(
        _compress_kernel,
        out_shape=jax.ShapeDtypeStruct((batch, 2, NSA_GROUPS, rows, LANES), BF16),
        grid=(batch, 2, NSA_GROUPS // 2),
        in_specs=[
            pl.BlockSpec((seq, LANES), lambda b, s, p: (b, cb + 2 * s + p)),
            pl.BlockSpec((None, CMP_STRIDE, NSA_HEAD_DIM, 2 * CMP_HIDDEN), lambda b, s, p: (s, 0, 0, 0)),
            pl.BlockSpec((None, CMP_STRIDE, SUBLANES, NSA_HEAD_DIM), lambda b, s, p: (s, 0, 0, 0)),
            pl.BlockSpec((None, CMP_HIDDEN, LANES), lambda b, s, p: (s, 0, 0)),
            pl.BlockSpec((1, LANES), lambda b, s, p: (0, 0)),
        ],
        out_specs=pl.BlockSpec((None, None, 2, rows, LANES), lambda b, s, p: (b, s, p, 0, 0)),
        scratch_shapes=[pltpu.VMEM((rows + SUBLANES, CMP_HIDDEN), F32)],
        compiler_params=_params(("parallel", "arbitrary", "arbitrary")),
        name="compress",
    )(z, w1r, pe8, w2p, gkc)


def _kv_pack_kernel(ks_ref, vs_ref, kw_ref, vw_ref, gks_ref, gkw_ref, ka_ref, vs_o, kw_o, vw_o):
    ts = ks_ref.shape[0]
    n_tiles = pl.num_programs(2) - 1
    i = pl.program_id(2)
    is_pad = i == n_tiles
    ksn = _split_halves(_half_norm(ks_ref[...], gks_ref[...]))
    kwn = _split_halves(_half_norm(kw_ref[...], gkw_ref[...]))
    vsh = _split_halves(vs_ref[...])
    vwh = _split_halves(vw_ref[...])
    pos = jnp.minimum(i, n_tiles - 1) * ts + lax.broadcasted_iota(jnp.int32, (ts, LANES), 0)
    blk = lax.broadcasted_iota(jnp.int32, (ts, LANES), 1)
    onehot = jnp.where(pos // SLC_LEN == blk, 1.0, 0.0)
    ones_col = jnp.where(blk == NSA_HEAD_DIM, 1.0, 0.0)
    for gg in range(2):
        ka_ref[gg] = jnp.concatenate([ksn[gg], onehot], axis=1).astype(ka_ref.dtype)
        vs_o[gg] = (vsh[gg] + ones_col).astype(vs_o.dtype)
        kw_o[gg] = jnp.where(is_pad, ones_col, kwn[gg]).astype(kw_o.dtype)
        vw_o[gg] = jnp.where(is_pad, 0.0, vwh[gg] + ones_col).astype(vw_o.dtype)


def _kv_pack(z, gks, gkw, batch, seq):
    assert PACK_TS == WIN
    nst = seq // PACK_TS
    cb = Z_KV // LANES + 4

    def zspec(k):
        return pl.BlockSpec((PACK_TS, LANES), lambda b, p, i: (b * nst + jnp.minimum(i, nst - 1), cb + 2 * k + p))

    def ospec(w):
        return pl.BlockSpec((None, 2, PACK_TS, w), lambda b, p, i: (b, p, jnp.minimum(i, nst - 1), 0))

    def wspec(w):
        return pl.BlockSpec((None, 2, PACK_TS, w), lambda b, p, i: (b, p, (i + 1) % (nst + 1), 0))

    def oshape(rows, w):
        return jax.ShapeDtypeStruct((batch, NSA_GROUPS, rows, w), BF16)

    gspec = pl.BlockSpec((1, LANES), lambda b, p, i: (0, 0))
    return pl.pallas_call(
        _kv_pack_kernel,
        out_shape=(oshape(seq, 2 * LANES), oshape(seq, LANES), oshape(seq + WIN, LANES), oshape(seq + WIN, LANES)),
        grid=(batch, NSA_GROUPS // 2, nst + 1),
        in_specs=[zspec(0), zspec(1), zspec(2), zspec(3), gspec, gspec],
        out_specs=(ospec(2 * LANES), ospec(LANES), wspec(LANES), wspec(LANES)),
        compiler_params=_params(("parallel", "arbitrary", "arbitrary")),
        name="kv_pack",
    )(z, z, z, z, gks, gkw)


def _count_outranking(score_ref, cnt_ref, n_live):
    nb, nq = score_ref.shape
    n_chunks = nb // SUBLANES
    cnt_ref[...] = jnp.zeros(cnt_ref.shape, cnt_ref.dtype)
    for ci in range(n_chunks):
        @pl.when(ci < n_live)
        def _():
            sub = lax.broadcasted_iota(jnp.int32, (SUBLANES, nq), 0)
            chunks = [score_ref[SUBLANES * v:SUBLANES * (v + 1), :] for v in range(n_chunks)]
            cnt = [cnt_ref[SUBLANES * v:SUBLANES * (v + 1), :] for v in range(n_chunks)]
            for r in range(SUBLANES):
                row = jnp.broadcast_to(chunks[ci][r:r + 1, :], (SUBLANES, nq))
                for v, c in enumerate(chunks):
                    if v < ci:
                        beats = row > c
                    elif v > ci:
                        beats = row >= c
                    else:
                        beats = ((sub < r) & (row > c)) | ((sub > r) & (row >= c))
                    cnt[v] = cnt[v] + jnp.where(beats, 1, 0)
            for v in range(n_chunks):
                cnt_ref[SUBLANES * v:SUBLANES * (v + 1), :] = cnt[v]


def _nsa_kernel(q_ref, gate_ref, gq_ref, ka_ref, vs_ref, kw_ref, vw_ref, kc_ref, vc_ref, map_ref, band_ref,
                o_ref, m_ref, acc_ref, sa_ref, sb_ref, mta_ref, mtb_ref, qa_ref, score_ref, cnt_ref, oc_ref, ow_ref):
    g = pl.program_id(1)
    t0 = pl.program_id(2) * (NSA_QPB * Q_BLOCK)
    sub = NSA_HPG * Q_BLOCK
    rows = NSA_QPB * sub
    n_q = NSA_QPB * Q_BLOCK
    n_cmp = kc_ref.shape[0]
    scale = NSA_HEAD_DIM ** -0.5 * math.log2(math.e)
    lane = lax.broadcasted_iota(jnp.int32, (1, LANES), 1)

    def block_rows(j):
        return slice(j * sub, (j + 1) * sub)

    heads = []
    for j in range(NSA_QPB):
        for pair in range(NSA_HPG // 2):
            qn = _half_norm(q_ref[j * Q_BLOCK:(j + 1) * Q_BLOCK, pair * LANES:(pair + 1) * LANES], gq_ref[...])
            heads.extend(_split_halves(qn * scale))
    q = jnp.where(lane == NSA_HEAD_DIM, NEG_INF, jnp.concatenate(heads, axis=0)).astype(BF16)
    row = lax.broadcasted_iota(jnp.int32, (rows, 1), 0)
    t = t0 + (row // sub) * Q_BLOCK + row % Q_BLOCK

    span = WIN + Q_BLOCK
    w_cols = span // LANES
    band = jnp.concatenate([band_ref[...]] * NSA_HPG, axis=0)

    def slab(j, c):
        k = j * w_cols + c
        per = (SEL_TK // LANES) * NSA_QPB
        ref = sa_ref if k < per else sb_ref
        k = k % per
        return ref.at[k // NSA_QPB, pl.ds((k % NSA_QPB) * sub, sub), :]

    for j in range(NSA_QPB):
        start = pl.multiple_of(t0 + j * Q_BLOCK, Q_BLOCK)
        s = _dot_nt(q[block_rows(j)], kw_ref[pl.ds(start, span), :]) + band
        mx = s[:, 0:LANES]
        for c in range(w_cols):
            slab(j, c)[...] = s[:, c * LANES:(c + 1) * LANES]
            if c:
                mx = jnp.maximum(mx, s[:, c * LANES:(c + 1) * LANES])
        mta_ref[block_rows(j), :] = jnp.broadcast_to(jnp.max(mx, axis=-1, keepdims=True), mx.shape)
    for j in range(NSA_QPB):
        start = pl.multiple_of(t0 + j * Q_BLOCK, Q_BLOCK)
        mt = mta_ref[block_rows(j), :]
        e = jnp.concatenate([jnp.exp2(slab(j, c)[...] - mt) for c in range(w_cols)], axis=1)
        ow_ref[block_rows(j), :] = _dot(e.astype(BF16), vw_ref[pl.ds(start, span), :])

    s = _dot_nt(q, kc_ref[...])
    cmp_end = lax.broadcasted_iota(jnp.int32, (1, n_cmp), 1) * CMP_STRIDE + (CMP_LEN - 1)
    valid = cmp_end <= t
    s = jnp.where(valid, s, NEG_INF)
    e = jnp.exp2(s - jnp.maximum(jnp.max(s, axis=-1, keepdims=True), 0.1 * NEG_INF))
    p = e / jnp.maximum(jnp.sum(e, axis=-1, keepdims=True), 1e-30)
    oc_ref[...] = _dot(p.astype(BF16), vc_ref[...])
    imps = []
    for j in range(NSA_QPB):
        imp = p[j * sub:j * sub + Q_BLOCK]
        for h in range(1, NSA_HPG):
            imp = imp + p[j * sub + h * Q_BLOCK:j * sub + (h + 1) * Q_BLOCK]
        imps.append(imp)
    imp = jnp.concatenate(imps, axis=0)
    hi = imp.astype(BF16)
    r1 = imp - hi.astype(F32)
    mid = r1.astype(BF16)
    lo = (r1 - mid.astype(F32)).astype(BF16)
    wmap = map_ref[...]
    imp_s = _dot(hi, wmap) + _dot(mid, wmap) + _dot(lo, wmap)
    tq = t0 + lax.broadcasted_iota(jnp.int32, (n_q, 1), 0)
    cur = tq // SLC_LEN
    blk = lax.broadcasted_iota(jnp.int32, (1, LANES), 1)
    forced = (blk == 0) | (blk == cur) | (blk == cur - 1)
    score = jnp.where(forced, FORCE_SCORE, jnp.where(blk <= cur, imp_s, NEG_INF))
    score_ref[...] = score.T
    _count_outranking(score_ref, cnt_ref, (t0 + n_q - 1) // SLC_LEN // SUBLANES + 1)
    chosen = jnp.where(cnt_ref[...] < SLC_TOP, 0.0, NEG_INF).T
    bias = jnp.where(blk <= cur, chosen, NEG_INF).astype(BF16)
    bias_rows = [bias[j * Q_BLOCK:(j + 1) * Q_BLOCK] for j in range(NSA_QPB) for _ in range(NSA_HPG)]
    qa_ref[...] = jnp.concatenate([q, jnp.concatenate(bias_rows, axis=0)], axis=1)

    n_cols = SEL_TK // LANES
    m_ref[...] = jnp.full(m_ref.shape, NEG_INF, F32)
    acc_ref[...] = jnp.zeros(acc_ref.shape, F32)

    def tile_max(s_ref):
        mx = s_ref[0]
        for c in range(1, n_cols):
            mx = jnp.maximum(mx, s_ref[c])
        return jnp.broadcast_to(jnp.max(mx, axis=-1, keepdims=True), mx.shape)

    def scores(kt, s_ref, mt_ref):
        k0 = pl.multiple_of(kt * SEL_TK, SEL_TK)
        s = _dot_nt(qa_ref[...], ka_ref[pl.ds(k0, SEL_TK), :])
        for c in range(n_cols):
            s_ref[c] = s[:, c * LANES:(c + 1) * LANES]
        mt_ref[...] = tile_max(s_ref)

    def update(kt, s_ref, mt_ref):
        k0 = pl.multiple_of(kt * SEL_TK, SEL_TK)
        m_prev = m_ref[...]
        m_new = jnp.maximum(m_prev, mt_ref[...])
        alpha = jnp.exp2(m_prev - m_new)
        p = jnp.concatenate([jnp.exp2(s_ref[c] - m_new) for c in range(n_cols)], axis=1)
        acc_ref[...] = alpha * acc_ref[...] + _dot(p.astype(BF16), vs_ref[pl.ds(k0, SEL_TK), :])
        m_ref[...] = m_new

    n_full = t0 // SEL_TK
    odd = n_full % 2

    @pl.when(odd == 0)
    def _():
        scores(0, sa_ref, mta_ref)

    @pl.when(odd == 1)
    def _():
        scores(0, sb_ref, mtb_ref)
        scores(1, sa_ref, mta_ref)
        update(0, sb_ref, mtb_ref)

    def pair(j, carry):
        kt = odd + 2 * j
        scores(kt + 1, sb_ref, mtb_ref)
        update(kt, sa_ref, mta_ref)
        scores(kt + 2, sa_ref, mta_ref)
        update(kt + 1, sb_ref, mtb_ref)
        return carry

    lax.fori_loop(0, n_full // 2, pair, 0)
    for j in range(NSA_QPB):
        rs = block_rows(j)
        tj = t0 + j * Q_BLOCK
        dcol = (tj % SEL_TK) // LANES
        kpos = tj + lax.broadcasted_iota(jnp.int32, (1, LANES), 1)
        sa_ref[dcol, rs, :] = jnp.where(kpos <= t[rs], sa_ref[dcol, rs, :], NEG_INF)
    mta_ref[...] = tile_max(sa_ref)
    update(n_full, sa_ref, mta_ref)

    def normalised(acc):
        denom = jnp.sum(jnp.where(lane == NSA_HEAD_DIM, acc, 0.0), axis=-1, keepdims=True)
        return jnp.where(lane < NSA_HEAD_DIM, acc / denom, 0.0)

    for j in range(NSA_QPB):
        qs = slice(j * Q_BLOCK, (j + 1) * Q_BLOCK)
        gates = _sigmoid(gate_ref[qs, :])
        outs = []
        for h in range(NSA_HPG):
            head = g * NSA_HPG + h
            rs = slice(j * sub + h * Q_BLOCK, j * sub + (h + 1) * Q_BLOCK)
            o = jnp.zeros((Q_BLOCK, LANES), F32)
            for r, branch in enumerate((oc_ref[rs, :], normalised(acc_ref[rs, :]), normalised(ow_ref[rs, :]))):
                gcol = jnp.sum(jnp.where(lane == r * NSA_HEADS + head, gates, 0.0), axis=-1, keepdims=True)
                o = o + gcol * branch
            outs.append(o)
        pairs = [outs[2 * k] + pltpu.roll(outs[2 * k + 1], NSA_HEAD_DIM, axis=1) for k in range(NSA_HPG // 2)]
        o_ref[qs, :] = jnp.concatenate(pairs, axis=1).astype(o_ref.dtype)


def _nsa(z, gq, kaug, vsel, kwin, vwin, kvc, wmap, batch, seq):
    n_q = NSA_QPB * Q_BLOCK
    nq = seq // n_q
    gw = NSA_HPG * NSA_HEAD_DIM
    rows = NSA_HPG * n_q
    n_cmp = seq // CMP_STRIDE

    def kvspec(w, pad=0):
        return pl.BlockSpec((None, None, seq + pad, w), lambda b, g, i: (b, g, 0, 0))

    def cspec(s):
        return pl.BlockSpec((None, None, None, n_cmp, LANES), lambda b, g, i: (b, s, g, 0, 0))

    r_idx = np.arange(Q_BLOCK)[:, None]
    c_idx = np.arange(WIN + Q_BLOCK)[None, :]
    band = jnp.asarray(np.where((c_idx > r_idx) & (c_idx <= r_idx + WIN), 0.0, NEG_INF), F32)

    return pl.pallas_call(
        _nsa_kernel,
        out_shape=jax.ShapeDtypeStruct((batch * seq, NSA_HEADS * NSA_HEAD_DIM), BF16),
        grid=(batch, NSA_GROUPS, nq),
        in_specs=[
            pl.BlockSpec((n_q, gw), lambda b, g, i: (b * nq + i, Z_QNSA // gw + g)),
            pl.BlockSpec((n_q, LANES), lambda b, g, i: (b * nq + i, Z_GATE // LANES)),
            pl.BlockSpec((1, LANES), lambda b, g, i: (0, 0)),
            kvspec(2 * LANES), kvspec(LANES), kvspec(LANES, WIN), kvspec(LANES, WIN),
            cspec(0), cspec(1),
            pl.BlockSpec((n_cmp, LANES), lambda b, g, i: (0, 0)),
            pl.BlockSpec((Q_BLOCK, WIN + Q_BLOCK), lambda b, g, i: (0, 0)),
        ],
        out_specs=pl.BlockSpec((n_q, gw), lambda b, g, i: (b * nq + i, g)),
        scratch_shapes=[
            pltpu.VMEM((rows, LANES), F32), pltpu.VMEM((rows, LANES), F32),
            pltpu.VMEM((SEL_TK // LANES, rows, LANES), F32), pltpu.VMEM((SEL_TK // LANES, rows, LANES), F32),
            pltpu.VMEM((rows, LANES), F32), pltpu.VMEM((rows, LANES), F32),
            pltpu.VMEM((rows, 2 * LANES), BF16),
            pltpu.VMEM((LANES, n_q), F32), pltpu.VMEM((LANES, n_q), jnp.int32),
            pltpu.VMEM((rows, LANES), F32), pltpu.VMEM((rows, LANES), F32)],
        compiler_params=_params(("parallel", "parallel", "arbitrary")),
        name="nsa",
    )(z, z, gq, kaug, vsel, kwin, vwin, kvc, kvc, wmap, band)


def _selection_weights(seq):
    n_c = (seq - CMP_LEN) // CMP_STRIDE + 1
    n_s = seq // SLC_LEN
    ratio = SLC_LEN // CMP_STRIDE
    w = np.zeros((seq // CMP_STRIDE, LANES), np.float32)
    for j in range(n_s):
        for off in range(-(CMP_LEN // CMP_STRIDE - 1), ratio):
            ci = ratio * j + off
            if 0 <= ci < n_c:
                cs = ci * CMP_STRIDE
                ov = min(cs + CMP_LEN, (j + 1) * SLC_LEN) - max(cs, j * SLC_LEN)
                w[ci, j] = max(ov, 0) / CMP_LEN
    return jnp.asarray(w, BF16)


def _merge_kernel(h_ref, xo_ref, yc_ref, yn_ref, ym_ref, wgc_ref, wgn_ref, wgm_ref, wb_ref, wo_ref, o_ref, mg_ref, *,
                  n_chunks):
    j = pl.program_id(1)

    @pl.when(j < n_chunks)
    def _():
        h = h_ref[...]
        merged = jnp.zeros((h.shape[0], wb_ref.shape[2]), F32)
        for n, (y_ref, wg_ref) in enumerate(((yc_ref, wgc_ref), (yn_ref, wgn_ref), (ym_ref, wgm_ref))):
            merged = merged + _sigmoid(_dot(h, wg_ref[...])) * _dot(y_ref[...], wb_ref[n])
        mg_ref[j] = merged.astype(BF16)

    @pl.when(j >= n_chunks)
    def _():
        acc = jnp.zeros(o_ref.shape, F32)
        for c in range(n_chunks):
            acc = acc + _dot(mg_ref[c], wo_ref[c])
        o_ref[...] = xo_ref[...] + acc


def _merge(x, h, yc, yn, ym, wg, wb, wo):
    n, d = x.shape
    bw = yc.shape[1]
    nc = d // MERGE_TN
    wo_c = wo.reshape(nc, MERGE_TN, d)

    def gspec(branch):
        return pl.BlockSpec((d, MERGE_TN), lambda i, j: (0, branch * nc + jnp.minimum(j, nc - 1)))

    return pl.pallas_call(
        functools.partial(_merge_kernel, n_chunks=nc),
        out_shape=jax.ShapeDtypeStruct((n, d), F32),
        grid=(n // MERGE_TM, 2 * nc),
        in_specs=[
            pl.BlockSpec((MERGE_TM, d), lambda i, j: (i, 0)),
            pl.BlockSpec((MERGE_TM, MERGE_TN), lambda i, j: (i, jnp.maximum(j - nc, 0))),
            pl.BlockSpec((MERGE_TM, bw), lambda i, j: (i, 0)),
            pl.BlockSpec((MERGE_TM, bw), lambda i, j: (i, 0)),
            pl.BlockSpec((MERGE_TM, bw), lambda i, j: (i, 0)),
            gspec(0), gspec(1), gspec(2),
            pl.BlockSpec((N_BRANCH, bw, MERGE_TN), lambda i, j: (0, 0, jnp.minimum(j, nc - 1))),
            pl.BlockSpec((nc, MERGE_TN, MERGE_TN), lambda i, j: (0, 0, jnp.maximum(j - nc, 0))),
        ],
        out_specs=pl.BlockSpec((MERGE_TM, MERGE_TN), lambda i, j: (i, jnp.maximum(j - nc, 0))),
        scratch_shapes=[pltpu.VMEM((nc, MERGE_TM, MERGE_TN), BF16)],
        compiler_params=_params(("parallel", "arbitrary")),
        name="merge",
    )(h, x, yc, yn, ym, wg, wg, wg, wb, wo_c)


def _layer(x2d, mem2d, batch, seq, mem_len, p):
    d = x2d.shape[1]
    bw = d // 2
    kvw = NSA_GROUPS * NSA_HEAD_DIM

    w_in = p['w_in'].astype(BF16)
    o_q = 3 * bw
    o_kv = o_q + NSA_HEADS * NSA_HEAD_DIM
    o_g = o_kv + 6 * kvw
    o_qm = o_g + 3 * NSA_HEADS
    o_gm = o_qm + bw
    gate_cols = w_in[:, o_g:o_qm].reshape(d, NSA_HEADS, 3).transpose(0, 2, 1).reshape(d, 3 * NSA_HEADS)
    w_proj = jnp.concatenate([
        w_in[:, :o_q], w_in[:, o_q:o_kv], w_in[:, o_qm:o_gm], w_in[:, o_kv:o_g], gate_cols,
        jnp.zeros((d, Z_COLS - Z_GATE - 3 * NSA_HEADS), BF16)], axis=1)
    w_gm = w_in[:, o_gm:]

    def cmp_weights(w1, w2, pe):
        w1l = w1.reshape(CMP_LEN, NSA_HEAD_DIM, CMP_HIDDEN)
        w1r = jnp.concatenate([w1l[:CMP_STRIDE], w1l[CMP_STRIDE:]], axis=2)
        pe8 = jnp.zeros((CMP_STRIDE, SUBLANES, NSA_HEAD_DIM), F32)
        pe8 = pe8.at[:, 0].set(pe[:CMP_STRIDE]).at[:, 1].set(pe[CMP_STRIDE:])
        w2p = jnp.pad(w2, ((0, 0), (0, LANES - NSA_HEAD_DIM)))
        return w1r.astype(BF16), pe8.astype(BF16), w2p.astype(BF16)

    ck = cmp_weights(p['cmp_w1_k'], p['cmp_w2_k'], p['cmp_pe_k'])
    cv = cmp_weights(p['cmp_w1_v'], p['cmp_w2_v'], p['cmp_pe_v'])
    w1r, pe8, w2p = (jnp.stack([a, b]) for a, b in zip(ck, cv))

    def row(v):
        return v.reshape(1, -1)

    def twice(v):
        return jnp.concatenate([v, v]).reshape(1, LANES)

    gkc = jnp.pad(p['nsa_kc_norm'], (0, LANES - NSA_HEAD_DIM)).reshape(1, LANES)

    x1 = _ffn(x2d, row(p['ffn1_norm']), p['ffn1_w_gate'].astype(BF16), p['ffn1_w_up'].astype(BF16),
              p['ffn1_w_down'].astype(BF16))
    z, h_mix = _in_proj(x1, row(p['mix_norm']), w_proj)
    y_conv = _conv(z, p['conv_w'], batch, seq)
    kvm = _mem_kv(mem2d, row(p['mem_norm']), p['w_mem_kv'].astype(BF16), row(p['mem_k_norm']))
    y_mem = _mem_attn(z, kvm, row(p['mem_q_norm']), batch, seq, mem_len)
    kvc = _compress(z, w1r, pe8, w2p, gkc, batch, seq)
    kaug, vsel, kwin, vwin = _kv_pack(z, twice(p['nsa_ks_norm']), twice(p['nsa_kw_norm']), batch, seq)
    y_nsa = _nsa(z, twice(p['nsa_q_norm']), kaug, vsel, kwin, vwin, kvc, _selection_weights(seq), batch, seq)
    x2 = _merge(x1, h_mix, y_conv, y_nsa, y_mem, w_gm, p['w_branch'].astype(BF16),
                p['w_o'].astype(BF16))
    return _ffn(x2, row(p['ffn2_norm']), p['ffn2_w_gate'].astype(BF16), p['ffn2_w_up'].astype(BF16),
                p['ffn2_w_down'].astype(BF16))


def kernel(x, mem, ffn1_norm, ffn1_w_gate, ffn1_w_up, ffn1_w_down, mix_norm, mem_norm, w_in, conv_w, nsa_q_norm, nsa_kc_norm, nsa_ks_norm, nsa_kw_norm, cmp_pe_k, cmp_w1_k, cmp_w2_k, cmp_pe_v, cmp_w1_v, cmp_w2_v, w_mem_kv, mem_q_norm, mem_k_norm, w_branch, w_o, ffn2_norm, ffn2_w_gate, ffn2_w_up, ffn2_w_down):
    batch, seq, d = x.shape
    mem_len = mem.shape[1]
    assert seq % (SEL_TK * 4) == 0 and seq // SLC_LEN <= LANES and seq >= WIN + Q_BLOCK
    params = dict(
        ffn1_norm=ffn1_norm, ffn1_w_gate=ffn1_w_gate, ffn1_w_up=ffn1_w_up, ffn1_w_down=ffn1_w_down,
        mix_norm=mix_norm, mem_norm=mem_norm, w_in=w_in, conv_w=conv_w, nsa_q_norm=nsa_q_norm,
        nsa_kc_norm=nsa_kc_norm, nsa_ks_norm=nsa_ks_norm, nsa_kw_norm=nsa_kw_norm, cmp_pe_k=cmp_pe_k,
        cmp_w1_k=cmp_w1_k, cmp_w2_k=cmp_w2_k, cmp_pe_v=cmp_pe_v, cmp_w1_v=cmp_w1_v, cmp_w2_v=cmp_w2_v,
        w_mem_kv=w_mem_kv, mem_q_norm=mem_q_norm, mem_k_norm=mem_k_norm, w_branch=w_branch, w_o=w_o,
        ffn2_norm=ffn2_norm, ffn2_w_gate=ffn2_w_gate, ffn2_w_up=ffn2_w_up, ffn2_w_down=ffn2_w_down)
    x2d = x.reshape(batch * seq, d)
    mem2d = mem.reshape(batch * mem_len, d)
    for l in range(ffn1_norm.shape[0]):
        x2d = _layer(x2d, mem2d, batch, seq, mem_len, {k: v[l] for k, v in params.items()})
    return x2d.reshape(batch, seq, d)
```

```python
import functools
import math

import jax
import jax.numpy as jnp
import numpy as np
from jax import lax
from jax.experimental import pallas as pl
from jax.experimental.pallas import tpu as pltpu

F32 = jnp.float32
BF16 = jnp.bfloat16

EPS = 1e-6
NEG_INF = -1e30
FORCE_SCORE = 1e9

CONV_K = 3
NSA_HEAD_DIM = 64
NSA_GROUPS = 4
NSA_HPG = 4
NSA_HEADS = NSA_GROUPS * NSA_HPG
CMP_LEN = 32
CMP_STRIDE = 16
CMP_HIDDEN = 4 * NSA_HEAD_DIM
SLC_LEN = 64
SLC_TOP = 16
WIN = 512
Q_BLOCK = 128
MEM_HEADS = 4
N_BRANCH = 3

LANES = 128
SUBLANES = 8
VMEM_LIMIT_BYTES = 56 * 1024 * 1024

FFN_TM = 512
FFN_TF = 512
PROJ_TM = 1024
PROJ_TN = 768
CONV_TS = 1024
MEM_TM = 1024
PACK_TS = 512
SEL_TK = 512
NSA_QPB = 4
MERGE_TM = 1024
MERGE_TN = 512
OUT_TM = 1024
OUT_TN = 1024

Z_CONV = 0
Z_QNSA = 3072
Z_QMEM = 4096
Z_KV = 5120
Z_GATE = 6656
Z_COLS = 6912


def _params(sem):
    return pltpu.CompilerParams(dimension_semantics=sem, vmem_limit_bytes=VMEM_LIMIT_BYTES)


def _rms(x, g):
    return x * lax.rsqrt(jnp.mean(x * x, axis=-1, keepdims=True) + EPS) * g


def _sigmoid(x):
    return 1.0 / (1.0 + jnp.exp(-x))


def _dot(a, b):
    return jnp.dot(a, b, preferred_element_type=F32)


def _dot_nt(a, b):
    return lax.dot_general(a, b, (((1,), (1,)), ((), ())), preferred_element_type=F32)


def _half_norm(x, g):
    lane = lax.broadcasted_iota(jnp.int32, x.shape, 1)
    lo = lane < NSA_HEAD_DIM
    xx = x * x
    s_lo = jnp.sum(jnp.where(lo, xx, 0.0), axis=-1, keepdims=True)
    s_hi = jnp.sum(jnp.where(lo, 0.0, xx), axis=-1, keepdims=True)
    r = jnp.where(lo, lax.rsqrt(s_lo / NSA_HEAD_DIM + EPS), lax.rsqrt(s_hi / NSA_HEAD_DIM + EPS))
    return x * r * g


def _split_halves(x):
    lane = lax.broadcasted_iota(jnp.int32, x.shape, 1)
    lo = lane < NSA_HEAD_DIM
    return jnp.where(lo, x, 0.0), jnp.where(lo, pltpu.roll(x, NSA_HEAD_DIM, axis=1), 0.0)


def _ffn_kernel(x_ref, g_ref, wg_ref, wu_ref, wd_ref, o_ref, h_ref):
    j = pl.program_id(1)

    @pl.when(j == 0)
    def _():
        h_ref[...] = _rms(x_ref[...], g_ref[...]).astype(BF16)
        o_ref[...] = jnp.zeros_like(o_ref)

    h = h_ref[...]
    gate = _dot(h, wg_ref[...])
    up = _dot(h, wu_ref[...])
    act = (gate * _sigmoid(gate) * up).astype(BF16)
    o_ref[...] += _dot(act, wd_ref[...])

    @pl.when(j == pl.num_programs(1) - 1)
    def _():
        o_ref[...] = x_ref[...] + 0.5 * o_ref[...]


def _ffn(x, g, wg, wu, wd):
    n, d = x.shape
    f = wg.shape[1]
    return pl.pallas_call(
        _ffn_kernel,
        out_shape=jax.ShapeDtypeStruct((n, d), F32),
        grid=(n // FFN_TM, f // FFN_TF),
        in_specs=[
            pl.BlockSpec((FFN_TM, d), lambda i, j: (i, 0)),
            pl.BlockSpec((1, d), lambda i, j: (0, 0)),
            pl.BlockSpec((d, FFN_TF), lambda i, j: (0, j)),
            pl.BlockSpec((d, FFN_TF), lambda i, j: (0, j)),
            pl.BlockSpec((FFN_TF, d), lambda i, j: (j, 0)),
        ],
        out_specs=pl.BlockSpec((FFN_TM, d), lambda i, j: (i, 0)),
        scratch_shapes=[pltpu.VMEM((FFN_TM, d), BF16)],
        compiler_params=_params(("parallel", "arbitrary")),
        name="ffn",
    )(x, g, wg, wu, wd)


def _proj_kernel(x_ref, g_ref, w_ref, o_ref, h_ref):
    @pl.when(pl.program_id(1) == 0)
    def _():
        h_ref[...] = _rms(x_ref[...], g_ref[...]).astype(BF16)

    o_ref[...] = _dot(h_ref[...], w_ref[...])


def _in_proj(x, g, w):
    n, d = x.shape
    c = w.shape[1]
    return pl.pallas_call(
        _proj_kernel,
        out_shape=(jax.ShapeDtypeStruct((n, c), F32), jax.ShapeDtypeStruct((n, d), BF16)),
        grid=(n // PROJ_TM, c // PROJ_TN),
        in_specs=[
            pl.BlockSpec((PROJ_TM, d), lambda i, j: (i, 0)),
            pl.BlockSpec((1, d), lambda i, j: (0, 0)),
            pl.BlockSpec((d, PROJ_TN), lambda i, j: (0, j)),
        ],
        out_specs=(pl.BlockSpec((PROJ_TM, PROJ_TN), lambda i, j: (i, j)),
                   pl.BlockSpec((PROJ_TM, d), lambda i, j: (i, 0))),
        compiler_params=_params(("parallel", "arbitrary")),
        name="in_proj",
    )(x, g, w)


def _conv_kernel(b_ref, c_ref, u_ref, cp_ref, up_ref, w_ref, o_ref, z_ref):
    ts = b_ref.shape[0]
    first = pl.program_id(1) == 0
    zp = cp_ref[...] * up_ref[...]
    z_ref[pl.ds(0, SUBLANES), :] = jnp.where(first, 0.0, zp)
    z = c_ref[...] * u_ref[...]
    z_ref[pl.ds(SUBLANES, ts), :] = z
    z1 = z_ref[pl.ds(SUBLANES - 1, ts), :]
    z2 = z_ref[pl.ds(SUBLANES - 2, ts), :]
    w = w_ref[...]
    y = b_ref[...] * (w[0:1, :] * z2 + w[1:2, :] * z1 + w[2:3, :] * z)
    o_ref[...] = y.astype(o_ref.dtype)


def _conv(z, conv_w, batch, seq):
    cw = conv_w.shape[1]
    nst = seq // CONV_TS
    rows_per = CONV_TS // SUBLANES
    cb = Z_CONV // cw

    def prev_map(col):
        return lambda b, i: (jnp.maximum((b * nst + i) * rows_per - 1, 0), col)

    return pl.pallas_call(
        _conv_kernel,
        out_shape=jax.ShapeDtypeStruct((batch * seq, cw), BF16),
        grid=(batch, nst),
        in_specs=[
            pl.BlockSpec((CONV_TS, cw), lambda b, i: (b * nst + i, cb)),
            pl.BlockSpec((CONV_TS, cw), lambda b, i: (b * nst + i, cb + 1)),
            pl.BlockSpec((CONV_TS, cw), lambda b, i: (b * nst + i, cb + 2)),
            pl.BlockSpec((SUBLANES, cw), prev_map(cb + 1)),
            pl.BlockSpec((SUBLANES, cw), prev_map(cb + 2)),
            pl.BlockSpec((CONV_K, cw), lambda b, i: (0, 0)),
        ],
        out_specs=pl.BlockSpec((CONV_TS, cw), lambda b, i: (b * nst + i, 0)),
        scratch_shapes=[pltpu.VMEM((CONV_TS + SUBLANES, cw), F32)],
        compiler_params=_params(("parallel", "arbitrary")),
        name="conv",
    )(z, z, z, z, z, conv_w)


def _mem_kv_kernel(m_ref, g_ref, w_ref, gk_ref, o_ref, *, n_key_tiles):
    h = _rms(m_ref[...], g_ref[...]).astype(BF16)
    kv = _dot(h, w_ref[...])
    is_key = pl.program_id(0) < n_key_tiles
    o_ref[...] = jnp.where(is_key, _rms(kv, gk_ref[...]), kv).astype(o_ref.dtype)


def _mem_kv(mem2d, g, w, gk):
    rows, d = mem2d.shape
    hd = gk.shape[1]
    cols = w.shape[1]
    return pl.pallas_call(
        functools.partial(_mem_kv_kernel, n_key_tiles=cols // (2 * hd)),
        out_shape=jax.ShapeDtypeStruct((rows, cols), BF16),
        grid=(cols // hd,),
        in_specs=[
            pl.BlockSpec((rows, d), lambda j: (0, 0)),
            pl.BlockSpec((1, d), lambda j: (0, 0)),
            pl.BlockSpec((d, hd), lambda j: (0, j)),
            pl.BlockSpec((1, hd), lambda j: (0, 0)),
        ],
        out_specs=pl.BlockSpec((rows, hd), lambda j: (0, j)),
        compiler_params=_params(("arbitrary",)),
        name="mem_kv",
    )(mem2d, g, w, gk)


def _mem_attn_kernel(q_ref, k_ref, v_ref, gq_ref, o_ref):
    hd = gq_ref.shape[1]
    scale = hd ** -0.5
    outs = []
    for h in range(MEM_HEADS):
        sl = slice(h * hd, (h + 1) * hd)
        q = (_rms(q_ref[:, sl], gq_ref[...]) * scale).astype(BF16)
        s = _dot_nt(q, k_ref[:, sl])
        e = jnp.exp(s - jnp.max(s, axis=-1, keepdims=True))
        p = e / jnp.sum(e, axis=-1, keepdims=True)
        outs.append(_dot(p.astype(BF16), v_ref[:, sl]))
    o_ref[...] = jnp.concatenate(outs, axis=1).astype(o_ref.dtype)


def _mem_attn(z, kvm, gq, batch, seq, mem_len):
    width = MEM_HEADS * gq.shape[1]
    nt = seq // MEM_TM
    return pl.pallas_call(
        _mem_attn_kernel,
        out_shape=jax.ShapeDtypeStruct((batch * seq, width), BF16),
        grid=(batch, nt),
        in_specs=[
            pl.BlockSpec((MEM_TM, width), lambda b, i: (b * nt + i, Z_QMEM // width)),
            pl.BlockSpec((mem_len, width), lambda b, i: (b, 0)),
            pl.BlockSpec((mem_len, width), lambda b, i: (b, 1)),
            pl.BlockSpec((1, gq.shape[1]), lambda b, i: (0, 0)),
        ],
        out_specs=pl.BlockSpec((MEM_TM, width), lambda b, i: (b * nt + i, 0)),
        compiler_params=_params(("parallel", "arbitrary")),
        name="mem_attn",
    )(z, kvm, kvm, gq)


def _compress_kernel(x_ref, w1_ref, pe_ref, w2_ref, g_ref, o_ref, b_ref):
    rows = o_ref.shape[1]
    is_key = pl.program_id(1) == 0
    ab = [jnp.zeros((rows, 2 * CMP_HIDDEN), F32) for _ in range(2)]
    peb = jnp.zeros((SUBLANES, 2 * CMP_HIDDEN), F32)
    for l in range(CMP_STRIDE):
        x = x_ref[pl.ds(l, rows, stride=CMP_STRIDE), :]
        w = w1_ref[l]
        halves = _split_halves(x)
        for gg in range(2):
            ab[gg] = ab[gg] + _dot(halves[gg][:, :NSA_HEAD_DIM].astype(BF16), w)
        peb = peb + _dot(pe_ref[l], w)
    bias = peb[0:1, :CMP_HIDDEN] + peb[1:2, CMP_HIDDEN:]
    b_ref[pl.ds(rows, SUBLANES), :] = jnp.zeros((SUBLANES, CMP_HIDDEN), F32)
    for gg in range(2):
        b_ref[pl.ds(0, rows), :] = ab[gg][:, CMP_HIDDEN:]
        hid = ab[gg][:, :CMP_HIDDEN] + b_ref[pl.ds(1, rows), :] + bias
        act = (hid * _sigmoid(hid)).astype(BF16)
        out = _dot(act, w2_ref[...])
        ms = jnp.sum(out * out, axis=-1, keepdims=True) / NSA_HEAD_DIM
        normed = out * lax.rsqrt(ms + EPS) * g_ref[...]
        o_ref[gg] = jnp.where(is_key, normed, out).astype(o_ref.dtype)


def _compress(z, w1r, pe8, w2p, gkc, batch, seq):
    rows = seq // CMP_STRIDE
    cb = Z_KV // LANES
    return pl.pallas_call(
        _compress_kernel,
        out_shape=jax.ShapeDtypeStruct((batch, 2, NSA_GROUPS, rows, LANES), BF16),
        grid=(batch, 2, NSA_GROUPS // 2),
        in_specs=[
            pl.BlockSpec((seq, LANES), lambda b, s, p: (b, cb + 2 * s + p)),
            pl.BlockSpec((None, CMP_STRIDE, NSA_HEAD_DIM, 2 * CMP_HIDDEN), lambda b, s, p: (s, 0, 0, 0)),
            pl.BlockSpec((None, CMP_STRIDE, SUBLANES, NSA_HEAD_DIM), lambda b, s, p: (s, 0, 0, 0)),
            pl.BlockSpec((None, CMP_HIDDEN, LANES), lambda b, s, p: (s, 0, 0)),
            pl.BlockSpec((1, LANES), lambda b, s, p: (0, 0)),
        ],
        out_specs=pl.BlockSpec((None, None, 2, rows, LANES), lambda b, s, p: (b, s, p, 0, 0)),
        scratch_shapes=[pltpu.VMEM((rows + SUBLANES, CMP_HIDDEN), F32)],
        compiler_params=_params(("parallel", "arbitrary", "arbitrary")),
        name="compress",
    )(z, w1r, pe8, w2p, gkc)


def _kv_pack_kernel(ks_ref, vs_ref, kw_ref, vw_ref, gks_ref, gkw_ref, ka_ref, vs_o, kw_o, vw_o):
    ts = ks_ref.shape[0]
    n_tiles = pl.num_programs(2) - 1
    i = pl.program_id(2)
    is_pad = i == n_tiles
    ksn = _split_halves(_half_norm(ks_ref[...], gks_ref[...]))
    kwn = _split_halves(_half_norm(kw_ref[...], gkw_ref[...]))
    vsh = _split_halves(vs_ref[...])
    vwh = _split_halves(vw_ref[...])
    pos = jnp.minimum(i, n_tiles - 1) * ts + lax.broadcasted_iota(jnp.int32, (ts, LANES), 0)
    blk = lax.broadcasted_iota(jnp.int32, (ts, LANES), 1)
    onehot = jnp.where(pos // SLC_LEN == blk, 1.0, 0.0)
    ones_col = jnp.where(blk == NSA_HEAD_DIM, 1.0, 0.0)
    for gg in range(2):
        ka_ref[gg] = jnp.concatenate([ksn[gg], onehot], axis=1).astype(ka_ref.dtype)
        vs_o[gg] = (vsh[gg] + ones_col).astype(vs_o.dtype)
        kw_o[gg] = jnp.where(is_pad, ones_col, kwn[gg]).astype(kw_o.dtype)
        vw_o[gg] = jnp.where(is_pad, 0.0, vwh[gg] + ones_col).astype(vw_o.dtype)


def _kv_pack(z, gks, gkw, batch, seq):
    assert PACK_TS == WIN
    nst = seq // PACK_TS
    cb = Z_KV // LANES + 4

    def zspec(k):
        return pl.BlockSpec((PACK_TS, LANES), lambda b, p, i: (b * nst + jnp.minimum(i, nst - 1), cb + 2 * k + p))

    def ospec(w):
        return pl.BlockSpec((None, 2, PACK_TS, w), lambda b, p, i: (b, p, jnp.minimum(i, nst - 1), 0))

    def wspec(w):
        return pl.BlockSpec((None, 2, PACK_TS, w), lambda b, p, i: (b, p, (i + 1) % (nst + 1), 0))

    def oshape(rows, w):
        return jax.ShapeDtypeStruct((batch, NSA_GROUPS, rows, w), BF16)

    gspec = pl.BlockSpec((1, LANES), lambda b, p, i: (0, 0))
    return pl.pallas_call(
        _kv_pack_kernel,
        out_shape=(oshape(seq, 2 * LANES), oshape(seq, LANES), oshape(seq + WIN, LANES), oshape(seq + WIN, LANES)),
        grid=(batch, NSA_GROUPS // 2, nst + 1),
        in_specs=[zspec(0), zspec(1), zspec(2), zspec(3), gspec, gspec],
        out_specs=(ospec(2 * LANES), ospec(LANES), wspec(LANES), wspec(LANES)),
        compiler_params=_params(("parallel", "arbitrary", "arbitrary")),
        name="kv_pack",
    )(z, z, z, z, gks, gkw)


def _count_outranking(score_ref, cnt_ref, n_live):
    nb, nq = score_ref.shape
    n_chunks = nb // SUBLANES
    cnt_ref[...] = jnp.zeros(cnt_ref.shape, cnt_ref.dtype)
    for ci in range(n_chunks):
        @pl.when(ci < n_live)
        def _():
            sub = lax.broadcasted_iota(jnp.int32, (SUBLANES, nq), 0)
            chunks = [score_ref[SUBLANES * v:SUBLANES * (v + 1), :] for v in range(n_chunks)]
            cnt = [cnt_ref[SUBLANES * v:SUBLANES * (v + 1), :] for v in range(n_chunks)]
            for r in range(SUBLANES):
                row = jnp.broadcast_to(chunks[ci][r:r + 1, :], (SUBLANES, nq))
                for v, c in enumerate(chunks):
                    if v < ci:
                        beats = row > c
                    elif v > ci:
                        beats = row >= c
                    else:
                        beats = ((sub < r) & (row > c)) | ((sub > r) & (row >= c))
                    cnt[v] = cnt[v] + jnp.where(beats, 1, 0)
            for v in range(n_chunks):
                cnt_ref[SUBLANES * v:SUBLANES * (v + 1), :] = cnt[v]


def _nsa_kernel(q_ref, gate_ref, gq_ref, ka_ref, vs_ref, kw_ref, vw_ref, kc_ref, vc_ref, map_ref, band_ref,
                o_ref, m_ref, acc_ref, sa_ref, sb_ref, mta_ref, mtb_ref, qa_ref, score_ref, cnt_ref, oc_ref, ow_ref):
    g = pl.program_id(1)
    t0 = pl.program_id(2) * (NSA_QPB * Q_BLOCK)
    sub = NSA_HPG * Q_BLOCK
    rows = NSA_QPB * sub
    n_q = NSA_QPB * Q_BLOCK
    n_cmp = kc_ref.shape[0]
    scale = NSA_HEAD_DIM ** -0.5 * math.log2(math.e)
    lane = lax.broadcasted_iota(jnp.int32, (1, LANES), 1)

    def block_rows(j):
        return slice(j * sub, (j + 1) * sub)

    heads = []
    for j in range(NSA_QPB):
        for pair in range(NSA_HPG // 2):
            qn = _half_norm(q_ref[j * Q_BLOCK:(j + 1) * Q_BLOCK, pair * LANES:(pair + 1) * LANES], gq_ref[...])
            heads.extend(_split_halves(qn * scale))
    q = jnp.where(lane == NSA_HEAD_DIM, NEG_INF, jnp.concatenate(heads, axis=0)).astype(BF16)
    row = lax.broadcasted_iota(jnp.int32, (rows, 1), 0)
    t = t0 + (row // sub) * Q_BLOCK + row % Q_BLOCK

    span = WIN + Q_BLOCK
    w_cols = span // LANES
    band = jnp.concatenate([band_ref[...]] * NSA_HPG, axis=0)

    def slab(j, c):
        k = j * w_cols + c
        per = (SEL_TK // LANES) * NSA_QPB
        ref = sa_ref if k < per else sb_ref
        k = k % per
        return ref.at[k // NSA_QPB, pl.ds((k % NSA_QPB) * sub, sub), :]

    for j in range(NSA_QPB):
        start = pl.multiple_of(t0 + j * Q_BLOCK, Q_BLOCK)
        s = _dot_nt(q[block_rows(j)], kw_ref[pl.ds(start, span), :]) + band
        mx = s[:, 0:LANES]
        for c in range(w_cols):
            slab(j, c)[...] = s[:, c * LANES:(c + 1) * LANES]
            if c:
                mx = jnp.maximum(mx, s[:, c * LANES:(c + 1) * LANES])
        mta_ref[block_rows(j), :] = jnp.broadcast_to(jnp.max(mx, axis=-1, keepdims=True), mx.shape)
    for j in range(NSA_QPB):
        start = pl.multiple_of(t0 + j * Q_BLOCK, Q_BLOCK)
        mt = mta_ref[block_rows(j), :]
        e = jnp.concatenate([jnp.exp2(slab(j, c)[...] - mt) for c in range(w_cols)], axis=1)
        ow_ref[block_rows(j), :] = _dot(e.astype(BF16), vw_ref[pl.ds(start, span), :])

    s = _dot_nt(q, kc_ref[...])
    cmp_end = lax.broadcasted_iota(jnp.int32, (1, n_cmp), 1) * CMP_STRIDE + (CMP_LEN - 1)
    valid = cmp_end <= t
    s = jnp.where(valid, s, NEG_INF)
    e = jnp.exp2(s - jnp.maximum(jnp.max(s, axis=-1, keepdims=True), 0.1 * NEG_INF))
    p = e / jnp.maximum(jnp.sum(e, axis=-1, keepdims=True), 1e-30)
    oc_ref[...] = _dot(p.astype(BF16), vc_ref[...])
    imps = []
    for j in range(NSA_QPB):
        imp = p[j * sub:j * sub + Q_BLOCK]
        for h in range(1, NSA_HPG):
            imp = imp + p[j * sub + h * Q_BLOCK:j * sub + (h + 1) * Q_BLOCK]
        imps.append(imp)
    imp = jnp.concatenate(imps, axis=0)
    hi = imp.astype(BF16)
    r1 = imp - hi.astype(F32)
    mid = r1.astype(BF16)
    lo = (r1 - mid.astype(F32)).astype(BF16)
    wmap = map_ref[...]
    imp_s = _dot(hi, wmap) + _dot(mid, wmap) + _dot(lo, wmap)
    tq = t0 + lax.broadcasted_iota(jnp.int32, (n_q, 1), 0)
    cur = tq // SLC_LEN
    blk = lax.broadcasted_iota(jnp.int32, (1, LANES), 1)
    forced = (blk == 0) | (blk == cur) | (blk == cur - 1)
    score = jnp.where(forced, FORCE_SCORE, jnp.where(blk <= cur, imp_s, NEG_INF))
    score_ref[...] = score.T
    _count_outranking(score_ref, cnt_ref, (t0 + n_q - 1) // SLC_LEN // SUBLANES + 1)
    chosen = jnp.where(cnt_ref[...] < SLC_TOP, 0.0, NEG_INF).T
    bias = jnp.where(blk <= cur, chosen, NEG_INF).astype(BF16)
    bias_rows = [bias[j * Q_BLOCK:(j + 1) * Q_BLOCK] for j in range(NSA_QPB) for _ in range(NSA_HPG)]
    qa_ref[...] = jnp.concatenate([q, jnp.concatenate(bias_rows, axis=0)], axis=1)

    n_cols = SEL_TK // LANES
    m_ref[...] = jnp.full(m_ref.shape, NEG_INF, F32)
    acc_ref[...] = jnp.zeros(acc_ref.shape, F32)

    def tile_max(s_ref):
        mx = s_ref[0]
        for c in range(1, n_cols):
            mx = jnp.maximum(mx, s_ref[c])
        return jnp.broadcast_to(jnp.max(mx, axis=-1, keepdims=True), mx.shape)

    def scores(kt, s_ref, mt_ref):
        k0 = pl.multiple_of(kt * SEL_TK, SEL_TK)
        s = _dot_nt(qa_ref[...], ka_ref[pl.ds(k0, SEL_TK), :])
        for c in range(n_cols):
            s_ref[c] = s[:, c * LANES:(c + 1) * LANES]
        mt_ref[...] = tile_max(s_ref)

    def update(kt, s_ref, mt_ref):
        k0 = pl.multiple_of(kt * SEL_TK, SEL_TK)
        m_prev = m_ref[...]
        m_new = jnp.maximum(m_prev, mt_ref[...])
        alpha = jnp.exp2(m_prev - m_new)
        p = jnp.concatenate([jnp.exp2(s_ref[c] - m_new) for c in range(n_cols)], axis=1)
        acc_ref[...] = alpha * acc_ref[...] + _dot(p.astype(BF16), vs_ref[pl.ds(k0, SEL_TK), :])
        m_ref[...] = m_new

    n_full = t0 // SEL_TK
    odd = n_full % 2

    @pl.when(odd == 0)
    def _():
        scores(0, sa_ref, mta_ref)

    @pl.when(odd == 1)
    def _():
        scores(0, sb_ref, mtb_ref)
        scores(1, sa_ref, mta_ref)
        update(0, sb_ref, mtb_ref)

    def pair(j, carry):
        kt = odd + 2 * j
        scores(kt + 1, sb_ref, mtb_ref)
        update(kt, sa_ref, mta_ref)
        scores(kt + 2, sa_ref, mta_ref)
        update(kt + 1, sb_ref, mtb_ref)
        return carry

    lax.fori_loop(0, n_full // 2, pair, 0)
    for j in range(NSA_QPB):
        rs = block_rows(j)
        tj = t0 + j * Q_BLOCK
        dcol = (tj % SEL_TK) // LANES
        kpos = tj + lax.broadcasted_iota(jnp.int32, (1, LANES), 1)
        sa_ref[dcol, rs, :] = jnp.where(kpos <= t[rs], sa_ref[dcol, rs, :], NEG_INF)
    mta_ref[...] = tile_max(sa_ref)
    update(n_full, sa_ref, mta_ref)

    def normalised(acc):
        denom = jnp.sum(jnp.where(lane == NSA_HEAD_DIM, acc, 0.0), axis=-1, keepdims=True)
        return jnp.where(lane < NSA_HEAD_DIM, acc / denom, 0.0)

    for j in range(NSA_QPB):
        qs = slice(j * Q_BLOCK, (j + 1) * Q_BLOCK)
        gates = _sigmoid(gate_ref[qs, :])
        outs = []
        for h in range(NSA_HPG):
            head = g * NSA_HPG + h
            rs = slice(j * sub + h * Q_BLOCK, j * sub + (h + 1) * Q_BLOCK)
            o = jnp.zeros((Q_BLOCK, LANES), F32)
            for r, branch in enumerate((oc_ref[rs, :], normalised(acc_ref[rs, :]), normalised(ow_ref[rs, :]))):
                gcol = jnp.sum(jnp.where(lane == r * NSA_HEADS + head, gates, 0.0), axis=-1, keepdims=True)
                o = o + gcol * branch
            outs.append(o)
        pairs = [outs[2 * k] + pltpu.roll(outs[2 * k + 1], NSA_HEAD_DIM, axis=1) for k in range(NSA_HPG // 2)]
        o_ref[qs, :] = jnp.concatenate(pairs, axis=1).astype(o_ref.dtype)


def _nsa(z, gq, kaug, vsel, kwin, vwin, kvc, wmap, batch, seq):
    n_q = NSA_QPB * Q_BLOCK
    nq = seq // n_q
    gw = NSA_HPG * NSA_HEAD_DIM
    rows = NSA_HPG * n_q
    n_cmp = seq // CMP_STRIDE

    def kvspec(w, pad=0):
        return pl.BlockSpec((None, None, seq + pad, w), lambda b, g, i: (b, g, 0, 0))

    def cspec(s):
        return pl.BlockSpec((None, None, None, n_cmp, LANES), lambda b, g, i: (b, s, g, 0, 0))

    r_idx = np.arange(Q_BLOCK)[:, None]
    c_idx = np.arange(WIN + Q_BLOCK)[None, :]
    band = jnp.asarray(np.where((c_idx > r_idx) & (c_idx <= r_idx + WIN), 0.0, NEG_INF), F32)

    return pl.pallas_call(
        _nsa_kernel,
        out_shape=jax.ShapeDtypeStruct((batch * seq, NSA_HEADS * NSA_HEAD_DIM), BF16),
        grid=(batch, NSA_GROUPS, nq),
        in_specs=[
            pl.BlockSpec((n_q, gw), lambda b, g, i: (b * nq + i, Z_QNSA // gw + g)),
            pl.BlockSpec((n_q, LANES), lambda b, g, i: (b * nq + i, Z_GATE // LANES)),
            pl.BlockSpec((1, LANES), lambda b, g, i: (0, 0)),
            kvspec(2 * LANES), kvspec(LANES), kvspec(LANES, WIN), kvspec(LANES, WIN),
            cspec(0), cspec(1),
            pl.BlockSpec((n_cmp, LANES), lambda b, g, i: (0, 0)),
            pl.BlockSpec((Q_BLOCK, WIN + Q_BLOCK), lambda b, g, i: (0, 0)),
        ],
        out_specs=pl.BlockSpec((n_q, gw), lambda b, g, i: (b * nq + i, g)),
        scratch_shapes=[
            pltpu.VMEM((rows, LANES), F32), pltpu.VMEM((rows, LANES), F32),
            pltpu.VMEM((SEL_TK // LANES, rows, LANES), F32), pltpu.VMEM((SEL_TK // LANES, rows, LANES), F32),
            pltpu.VMEM((rows, LANES), F32), pltpu.VMEM((rows, LANES), F32),
            pltpu.VMEM((rows, 2 * LANES), BF16),
            pltpu.VMEM((LANES, n_q), F32), pltpu.VMEM((LANES, n_q), jnp.int32),
            pltpu.VMEM((rows, LANES), F32), pltpu.VMEM((rows, LANES), F32)],
        compiler_params=_params(("parallel", "parallel", "arbitrary")),
        name="nsa",
    )(z, z, gq, kaug, vsel, kwin, vwin, kvc, kvc, wmap, band)


def _selection_weights(seq):
    n_c = (seq - CMP_LEN) // CMP_STRIDE + 1
    n_s = seq // SLC_LEN
    ratio = SLC_LEN // CMP_STRIDE
    w = np.zeros((seq // CMP_STRIDE, LANES), np.float32)
    for j in range(n_s):
        for off in range(-(CMP_LEN // CMP_STRIDE - 1), ratio):
            ci = ratio * j + off
            if 0 <= ci < n_c:
                cs = ci * CMP_STRIDE
                ov = min(cs + CMP_LEN, (j + 1) * SLC_LEN) - max(cs, j * SLC_LEN)
                w[ci, j] = max(ov, 0) / CMP_LEN
    return jnp.asarray(w, BF16)


def _merge_kernel(h_ref, yc_ref, yn_ref, ym_ref, wgc_ref, wgn_ref, wgm_ref, wb_ref, o_ref):
    h = h_ref[...]
    merged = jnp.zeros(o_ref.shape, F32)
    for n, (y_ref, wg_ref) in enumerate(((yc_ref, wgc_ref), (yn_ref, wgn_ref), (ym_ref, wgm_ref))):
        merged = merged + _sigmoid(_dot(h, wg_ref[...])) * _dot(y_ref[...], wb_ref[n])
    o_ref[...] = merged.astype(o_ref.dtype)


def _merge(h, yc, yn, ym, wg, wb):
    n, d = h.shape
    bw = yc.shape[1]
    nc = d // MERGE_TN

    def gspec(branch):
        return pl.BlockSpec((d, MERGE_TN), lambda i, j: (0, branch * nc + j))

    return pl.pallas_call(
        _merge_kernel,
        out_shape=jax.ShapeDtypeStruct((n, d), BF16),
        grid=(n // MERGE_TM, nc),
        in_specs=[
            pl.BlockSpec((MERGE_TM, d), lambda i, j: (i, 0)),
            pl.BlockSpec((MERGE_TM, bw), lambda i, j: (i, 0)),
            pl.BlockSpec((MERGE_TM, bw), lambda i, j: (i, 0)),
            pl.BlockSpec((MERGE_TM, bw), lambda i, j: (i, 0)),
            gspec(0), gspec(1), gspec(2),
            pl.BlockSpec((N_BRANCH, bw, MERGE_TN), lambda i, j: (0, 0, j)),
        ],
        out_specs=pl.BlockSpec((MERGE_TM, MERGE_TN), lambda i, j: (i, j)),
        compiler_params=_params(("parallel", "arbitrary")),
        name="merge",
    )(h, yc, yn, ym, wg, wg, wg, wb)


def _out_proj_kernel(x_ref, m_ref, w_ref, o_ref):
    o_ref[...] = x_ref[...] + _dot(m_ref[...], w_ref[...])


def _out_proj(x, merged, wo):
    n, d = x.shape
    return pl.pallas_call(
        _out_proj_kernel,
        out_shape=jax.ShapeDtypeStruct((n, d), F32),
        grid=(n // OUT_TM, d // OUT_TN),
        in_specs=[
            pl.BlockSpec((OUT_TM, OUT_TN), lambda i, j: (i, j)),
            pl.BlockSpec((OUT_TM, d), lambda i, j: (i, 0)),
            pl.BlockSpec((d, OUT_TN), lambda i, j: (0, j)),
        ],
        out_specs=pl.BlockSpec((OUT_TM, OUT_TN), lambda i, j: (i, j)),
        compiler_params=_params(("parallel", "arbitrary")),
        name="out_proj",
    )(x, merged, wo)


def _layer(x2d, mem2d, batch, seq, mem_len, p):
    d = x2d.shape[1]
    bw = d // 2
    kvw = NSA_GROUPS * NSA_HEAD_DIM

    w_in = p['w_in'].astype(BF16)
    o_q = 3 * bw
    o_kv = o_q + NSA_HEADS * NSA_HEAD_DIM
    o_g = o_kv + 6 * kvw
    o_qm = o_g + 3 * NSA_HEADS
    o_gm = o_qm + bw
    gate_cols = w_in[:, o_g:o_qm].reshape(d, NSA_HEADS, 3).transpose(0, 2, 1).reshape(d, 3 * NSA_HEADS)
    w_proj = jnp.concatenate([
        w_in[:, :o_q], w_in[:, o_q:o_kv], w_in[:, o_qm:o_gm], w_in[:, o_kv:o_g], gate_cols,
        jnp.zeros((d, Z_COLS - Z_GATE - 3 * NSA_HEADS), BF16)], axis=1)
    w_gm = w_in[:, o_gm:]

    def cmp_weights(w1, w2, pe):
        w1l = w1.reshape(CMP_LEN, NSA_HEAD_DIM, CMP_HIDDEN)
        w1r = jnp.concatenate([w1l[:CMP_STRIDE], w1l[CMP_STRIDE:]], axis=2)
        pe8 = jnp.zeros((CMP_STRIDE, SUBLANES, NSA_HEAD_DIM), F32)
        pe8 = pe8.at[:, 0].set(pe[:CMP_STRIDE]).at[:, 1].set(pe[CMP_STRIDE:])
        w2p = jnp.pad(w2, ((0, 0), (0, LANES - NSA_HEAD_DIM)))
        return w1r.astype(BF16), pe8.astype(BF16), w2p.astype(BF16)

    ck = cmp_weights(p['cmp_w1_k'], p['cmp_w2_k'], p['cmp_pe_k'])
    cv = cmp_weights(p['cmp_w1_v'], p['cmp_w2_v'], p['cmp_pe_v'])
    w1r, pe8, w2p = (jnp.stack([a, b]) for a, b in zip(ck, cv))

    def row(v):
        return v.reshape(1, -1)

    def twice(v):
        return jnp.concatenate([v, v]).reshape(1, LANES)

    gkc = jnp.pad(p['nsa_kc_norm'], (0, LANES - NSA_HEAD_DIM)).reshape(1, LANES)

    x1 = _ffn(x2d, row(p['ffn1_norm']), p['ffn1_w_gate'].astype(BF16), p['ffn1_w_up'].astype(BF16),
              p['ffn1_w_down'].astype(BF16))
    z, h_mix = _in_proj(x1, row(p['mix_norm']), w_proj)
    y_conv = _conv(z, p['conv_w'], batch, seq)
    kvm = _mem_kv(mem2d, row(p['mem_norm']), p['w_mem_kv'].astype(BF16), row(p['mem_k_norm']))
    y_mem = _mem_attn(z, kvm, row(p['mem_q_norm']), batch, seq, mem_len)
    kvc = _compress(z, w1r, pe8, w2p, gkc, batch, seq)
    kaug, vsel, kwin, vwin = _kv_pack(z, twice(p['nsa_ks_norm']), twice(p['nsa_kw_norm']), batch, seq)
    y_nsa = _nsa(z, twice(p['nsa_q_norm']), kaug, vsel, kwin, vwin, kvc, _selection_weights(seq), batch, seq)
    merged = _merge(h_mix, y_conv, y_nsa, y_mem, w_gm, p['w_branch'].astype(BF16))
    x2 = _out_proj(x1, merged, p['w_o'].astype(BF16))
    return _ffn(x2, row(p['ffn2_norm']), p['ffn2_w_gate'].astype(BF16), p['ffn2_w_up'].astype(BF16),
                p['ffn2_w_down'].astype(BF16))


def kernel(x, mem, ffn1_norm, ffn1_w_gate, ffn1_w_up, ffn1_w_down, mix_norm, mem_norm, w_in, conv_w, nsa_q_norm, nsa_kc_norm, nsa_ks_norm, nsa_kw_norm, cmp_pe_k, cmp_w1_k, cmp_w2_k, cmp_pe_v, cmp_w1_v, cmp_w2_v, w_mem_kv, mem_q_norm, mem_k_norm, w_branch, w_o, ffn2_norm, ffn2_w_gate, ffn2_w_up, ffn2_w_down):
    batch, seq, d = x.shape
    mem_len = mem.shape[1]
    assert seq % (SEL_TK * 4) == 0 and seq // SLC_LEN <= LANES and seq >= WIN + Q_BLOCK
    params = dict(
        ffn1_norm=ffn1_norm, ffn1_w_gate=ffn1_w_gate, ffn1_w_up=ffn1_w_up, ffn1_w_down=ffn1_w_down,
        mix_norm=mix_norm, mem_norm=mem_norm, w_in=w_in, conv_w=conv_w, nsa_q_norm=nsa_q_norm,
        nsa_kc_norm=nsa_kc_norm, nsa_ks_norm=nsa_ks_norm, nsa_kw_norm=nsa_kw_norm, cmp_pe_k=cmp_pe_k,
        cmp_w1_k=cmp_w1_k, cmp_w2_k=cmp_w2_k, cmp_pe_v=cmp_pe_v, cmp_w1_v=cmp_w1_v, cmp_w2_v=cmp_w2_v,
        w_mem_kv=w_mem_kv, mem_q_norm=mem_q_norm, mem_k_norm=mem_k_norm, w_branch=w_branch, w_o=w_o,
        ffn2_norm=ffn2_norm, ffn2_w_gate=ffn2_w_gate, ffn2_w_up=ffn2_w_up, ffn2_w_down=ffn2_w_down)
    x2d = x.reshape(batch * seq, d)
    mem2d = mem.reshape(batch * mem_len, d)
    for l in range(ffn1_norm.shape[0]):
        x2d = _layer(x2d, mem2d, batch, seq, mem_len, {k: v[l] for k, v in params.items()})
    return x2d.reshape(batch, seq, d)
```

```python
import functools
import math

import jax
import jax.numpy as jnp
import numpy as np
from jax import lax
from jax.experimental import pallas as pl
from jax.experimental.pallas import tpu as pltpu

F32 = jnp.float32
BF16 = jnp.bfloat16

EPS = 1e-6
NEG_INF = -1e30
FORCE_SCORE = 1e9

CONV_K = 3
NSA_HEAD_DIM = 64
NSA_GROUPS = 4
NSA_HPG = 4
NSA_HEADS = NSA_GROUPS * NSA_HPG
CMP_LEN = 32
CMP_STRIDE = 16
CMP_HIDDEN = 4 * NSA_HEAD_DIM
SLC_LEN = 64
SLC_TOP = 16
WIN = 512
Q_BLOCK = 128
MEM_HEADS = 4
N_BRANCH = 3

LANES = 128
SUBLANES = 8
VMEM_LIMIT_BYTES = 56 * 1024 * 1024

FFN_TM = 512
FFN_TF = 512
PROJ_TM = 1024
PROJ_TN = 768
CONV_TS = 1024
MEM_TM = 1024
PACK_TS = 512
SEL_TK = 512
NSA_QPB = 4
MERGE_TM = 1024
MERGE_TN = 512
OUT_TM = 1024
OUT_TN = 1024

Z_CONV = 0
Z_QNSA = 3072
Z_QMEM = 4096
Z_KV = 5120
Z_GATE = 6656
Z_COLS = 6912


def _params(sem):
    return pltpu.CompilerParams(dimension_semantics=sem, vmem_limit_bytes=VMEM_LIMIT_BYTES)


def _rms(x, g):
    return x * lax.rsqrt(jnp.mean(x * x, axis=-1, keepdims=True) + EPS) * g


def _sigmoid(x):
    return 1.0 / (1.0 + jnp.exp(-x))


def _dot(a, b):
    return jnp.dot(a, b, preferred_element_type=F32)


def _dot_nt(a, b):
    return lax.dot_general(a, b, (((1,), (1,)), ((), ())), preferred_element_type=F32)


def _half_norm(x, g):
    lane = lax.broadcasted_iota(jnp.int32, x.shape, 1)
    lo = lane < NSA_HEAD_DIM
    xx = x * x
    s_lo = jnp.sum(jnp.where(lo, xx, 0.0), axis=-1, keepdims=True)
    s_hi = jnp.sum(jnp.where(lo, 0.0, xx), axis=-1, keepdims=True)
    r = jnp.where(lo, lax.rsqrt(s_lo / NSA_HEAD_DIM + EPS), lax.rsqrt(s_hi / NSA_HEAD_DIM + EPS))
    return x * r * g


def _split_halves(x):
    lane = lax.broadcasted_iota(jnp.int32, x.shape, 1)
    lo = lane < NSA_HEAD_DIM
    return jnp.where(lo, x, 0.0), jnp.where(lo, pltpu.roll(x, NSA_HEAD_DIM, axis=1), 0.0)


def _ffn_kernel(x_ref, g_ref, wg_ref, wu_ref, wd_ref, o_ref, h_ref):
    j = pl.program_id(1)

    @pl.when(j == 0)
    def _():
        h_ref[...] = _rms(x_ref[...], g_ref[...]).astype(BF16)
        o_ref[...] = jnp.zeros_like(o_ref)

    h = h_ref[...]
    gate = _dot(h, wg_ref[...])
    up = _dot(h, wu_ref[...])
    act = (gate * _sigmoid(gate) * up).astype(BF16)
    o_ref[...] += _dot(act, wd_ref[...])

    @pl.when(j == pl.num_programs(1) - 1)
    def _():
        o_ref[...] = x_ref[...] + 0.5 * o_ref[...]


def _ffn(x, g, wg, wu, wd):
    n, d = x.shape
    f = wg.shape[1]
    return pl.pallas_call(
        _ffn_kernel,
        out_shape=jax.ShapeDtypeStruct((n, d), F32),
        grid=(n // FFN_TM, f // FFN_TF),
        in_specs=[
            pl.BlockSpec((FFN_TM, d), lambda i, j: (i, 0)),
            pl.BlockSpec((1, d), lambda i, j: (0, 0)),
            pl.BlockSpec((d, FFN_TF), lambda i, j: (0, j)),
            pl.BlockSpec((d, FFN_TF), lambda i, j: (0, j)),
            pl.BlockSpec((FFN_TF, d), lambda i, j: (j, 0)),
        ],
        out_specs=pl.BlockSpec((FFN_TM, d), lambda i, j: (i, 0)),
        scratch_shapes=[pltpu.VMEM((FFN_TM, d), BF16)],
        compiler_params=_params(("parallel", "arbitrary")),
        name="ffn",
    )(x, g, wg, wu, wd)


def _proj_kernel(x_ref, g_ref, w_ref, o_ref, h_ref):
    @pl.when(pl.program_id(1) == 0)
    def _():
        h_ref[...] = _rms(x_ref[...], g_ref[...]).astype(BF16)

    o_ref[...] = _dot(h_ref[...], w_ref[...])


def _in_proj(x, g, w):
    n, d = x.shape
    c = w.shape[1]
    return pl.pallas_call(
        _proj_kernel,
        out_shape=(jax.ShapeDtypeStruct((n, c), F32), jax.ShapeDtypeStruct((n, d), BF16)),
        grid=(n // PROJ_TM, c // PROJ_TN),
        in_specs=[
            pl.BlockSpec((PROJ_TM, d), lambda i, j: (i, 0)),
            pl.BlockSpec((1, d), lambda i, j: (0, 0)),
            pl.BlockSpec((d, PROJ_TN), lambda i, j: (0, j)),
        ],
        out_specs=(pl.BlockSpec((PROJ_TM, PROJ_TN), lambda i, j: (i, j)),
                   pl.BlockSpec((PROJ_TM, d), lambda i, j: (i, 0))),
        compiler_params=_params(("parallel", "arbitrary")),
        name="in_proj",
    )(x, g, w)


def _conv_kernel(b_ref, c_ref, u_ref, cp_ref, up_ref, w_ref, o_ref, z_ref):
    ts = b_ref.shape[0]
    first = pl.program_id(1) == 0
    zp = cp_ref[...] * up_ref[...]
    z_ref[pl.ds(0, SUBLANES), :] = jnp.where(first, 0.0, zp)
    z = c_ref[...] * u_ref[...]
    z_ref[pl.ds(SUBLANES, ts), :] = z
    z1 = z_ref[pl.ds(SUBLANES - 1, ts), :]
    z2 = z_ref[pl.ds(SUBLANES - 2, ts), :]
    w = w_ref[...]
    y = b_ref[...] * (w[0:1, :] * z2 + w[1:2, :] * z1 + w[2:3, :] * z)
    o_ref[...] = y.astype(o_ref.dtype)


def _conv(z, conv_w, batch, seq):
    cw = conv_w.shape[1]
    nst = seq // CONV_TS
    rows_per = CONV_TS // SUBLANES
    cb = Z_CONV // cw

    def prev_map(col):
        return lambda b, i: (jnp.maximum((b * nst + i) * rows_per - 1, 0), col)

    return pl.pallas_call(
        _conv_kernel,
        out_shape=jax.ShapeDtypeStruct((batch * seq, cw), BF16),
        grid=(batch, nst),
        in_specs=[
            pl.BlockSpec((CONV_TS, cw), lambda b, i: (b * nst + i, cb)),
            pl.BlockSpec((CONV_TS, cw), lambda b, i: (b * nst + i, cb + 1)),
            pl.BlockSpec((CONV_TS, cw), lambda b, i: (b * nst + i, cb + 2)),
            pl.BlockSpec((SUBLANES, cw), prev_map(cb + 1)),
            pl.BlockSpec((SUBLANES, cw), prev_map(cb + 2)),
            pl.BlockSpec((CONV_K, cw), lambda b, i: (0, 0)),
        ],
        out_specs=pl.BlockSpec((CONV_TS, cw), lambda b, i: (b * nst + i, 0)),
        scratch_shapes=[pltpu.VMEM((CONV_TS + SUBLANES, cw), F32)],
        compiler_params=_params(("parallel", "arbitrary")),
        name="conv",
    )(z, z, z, z, z, conv_w)


def _mem_kv_kernel(m_ref, g_ref, w_ref, gk_ref, o_ref, *, n_key_tiles):
    h = _rms(m_ref[...], g_ref[...]).astype(BF16)
    kv = _dot(h, w_ref[...])
    is_key = pl.program_id(0) < n_key_tiles
    o_ref[...] = jnp.where(is_key, _rms(kv, gk_ref[...]), kv).astype(o_ref.dtype)


def _mem_kv(mem2d, g, w, gk):
    rows, d = mem2d.shape
    hd = gk.shape[1]
    cols = w.shape[1]
    return pl.pallas_call(
        functools.partial(_mem_kv_kernel, n_key_tiles=cols // (2 * hd)),
        out_shape=jax.ShapeDtypeStruct((rows, cols), BF16),
        grid=(cols // hd,),
        in_specs=[
            pl.BlockSpec((rows, d), lambda j: (0, 0)),
            pl.BlockSpec((1, d), lambda j: (0, 0)),
            pl.BlockSpec((d, hd), lambda j: (0, j)),
            pl.BlockSpec((1, hd), lambda j: (0, 0)),
        ],
        out_specs=pl.BlockSpec((rows, hd), lambda j: (0, j)),
        compiler_params=_params(("arbitrary",)),
        name="mem_kv",
    )(mem2d, g, w, gk)


def _mem_attn_kernel(q_ref, k_ref, v_ref, gq_ref, o_ref):
    hd = gq_ref.shape[1]
    scale = hd ** -0.5
    outs = []
    for h in range(MEM_HEADS):
        sl = slice(h * hd, (h + 1) * hd)
        q = (_rms(q_ref[:, sl], gq_ref[...]) * scale).astype(BF16)
        s = _dot_nt(q, k_ref[:, sl])
        e = jnp.exp(s - jnp.max(s, axis=-1, keepdims=True))
        p = e / jnp.sum(e, axis=-1, keepdims=True)
        outs.append(_dot(p.astype(BF16), v_ref[:, sl]))
    o_ref[...] = jnp.concatenate(outs, axis=1).astype(o_ref.dtype)


def _mem_attn(z, kvm, gq, batch, seq, mem_len):
    width = MEM_HEADS * gq.shape[1]
    nt = seq // MEM_TM
    return pl.pallas_call(
        _mem_attn_kernel,
        out_shape=jax.ShapeDtypeStruct((batch * seq, width), BF16),
        grid=(batch, nt),
        in_specs=[
            pl.BlockSpec((MEM_TM, width), lambda b, i: (b * nt + i, Z_QMEM // width)),
            pl.BlockSpec((mem_len, width), lambda b, i: (b, 0)),
            pl.BlockSpec((mem_len, width), lambda b, i: (b, 1)),
            pl.BlockSpec((1, gq.shape[1]), lambda b, i: (0, 0)),
        ],
        out_specs=pl.BlockSpec((MEM_TM, width), lambda b, i: (b * nt + i, 0)),
        compiler_params=_params(("parallel", "arbitrary")),
        name="mem_attn",
    )(z, kvm, kvm, gq)


def _compress_kernel(x_ref, w1_ref, pe_ref, w2_ref, g_ref, o_ref, b_ref):
    rows = o_ref.shape[1]
    is_key = pl.program_id(1) == 0
    ab = [jnp.zeros((rows, 2 * CMP_HIDDEN), F32) for _ in range(2)]
    peb = jnp.zeros((SUBLANES, 2 * CMP_HIDDEN), F32)
    for l in range(CMP_STRIDE):
        x = x_ref[pl.ds(l, rows, stride=CMP_STRIDE), :]
        w = w1_ref[l]
        halves = _split_halves(x)
        for gg in range(2):
            ab[gg] = ab[gg] + _dot(halves[gg][:, :NSA_HEAD_DIM].astype(BF16), w)
        peb = peb + _dot(pe_ref[l], w)
    bias = peb[0:1, :CMP_HIDDEN] + peb[1:2, CMP_HIDDEN:]
    b_ref[pl.ds(rows, SUBLANES), :] = jnp.zeros((SUBLANES, CMP_HIDDEN), F32)
    for gg in range(2):
        b_ref[pl.ds(0, rows), :] = ab[gg][:, CMP_HIDDEN:]
        hid = ab[gg][:, :CMP_HIDDEN] + b_ref[pl.ds(1, rows), :] + bias
        act = (hid * _sigmoid(hid)).astype(BF16)
        out = _dot(act, w2_ref[...])
        ms = jnp.sum(out * out, axis=-1, keepdims=True) / NSA_HEAD_DIM
        normed = out * lax.rsqrt(ms + EPS) * g_ref[...]
        o_ref[gg] = jnp.where(is_key, normed, out).astype(o_ref.dtype)


def _compress(z, w1r, pe8, w2p, gkc, batch, seq):
    rows = seq // CMP_STRIDE
    cb = Z_KV // LANES
    return pl.pallas_call(
        _compress_kernel,
        out_shape=jax.ShapeDtypeStruct((batch, 2, NSA_GROUPS, rows, LANES), BF16),
        grid=(batch, 2, NSA_GROUPS // 2),
        in_specs=[
            pl.BlockSpec((seq, LANES), lambda b, s, p: (b, cb + 2 * s + p)),
            pl.BlockSpec((None, CMP_STRIDE, NSA_HEAD_DIM, 2 * CMP_HIDDEN), lambda b, s, p: (s, 0, 0, 0)),
            pl.BlockSpec((None, CMP_STRIDE, SUBLANES, NSA_HEAD_DIM), lambda b, s, p: (s, 0, 0, 0)),
            pl.BlockSpec((None, CMP_HIDDEN, LANES), lambda b, s, p: (s, 0, 0)),
            pl.BlockSpec((1, LANES), lambda b, s, p: (0, 0)),
        ],
        out_specs=pl.BlockSpec((None, None, 2, rows, LANES), lambda b, s, p: (b, s, p, 0, 0)),
        scratch_shapes=[pltpu.VMEM((rows + SUBLANES, CMP_HIDDEN), F32)],
        compiler_params=_params(("parallel", "arbitrary", "arbitrary")),
        name="compress",
    )(z, w1r, pe8, w2p, gkc)


def _kv_pack_kernel(ks_ref, vs_ref, kw_ref, vw_ref, gks_ref, gkw_ref, ka_ref, vs_o, kw_o, vw_o):
    ts = ks_ref.shape[0]
    n_tiles = pl.num_programs(2) - 1
    i = pl.program_id(2)
    is_pad = i == n_tiles
    ksn = _split_halves(_half_norm(ks_ref[...], gks_ref[...]))
    kwn = _split_halves(_half_norm(kw_ref[...], gkw_ref[...]))
    vsh = _split_halves(vs_ref[...])
    vwh = _split_halves(vw_ref[...])
    pos = jnp.minimum(i, n_tiles - 1) * ts + lax.broadcasted_iota(jnp.int32, (ts, LANES), 0)
    blk = lax.broadcasted_iota(jnp.int32, (ts, LANES), 1)
    onehot = jnp.where(jnp.right_shift(pos, SLC_LEN.bit_length() - 1) == blk, 1.0, 0.0)
    ones_col = jnp.where(blk == NSA_HEAD_DIM, 1.0, 0.0)
    for gg in range(2):
        ka_ref[gg] = jnp.concatenate([ksn[gg], onehot], axis=1).astype(ka_ref.dtype)
        vs_o[gg] = (vsh[gg] + ones_col).astype(vs_o.dtype)
        kw_o[gg] = jnp.where(is_pad, ones_col, kwn[gg]).astype(kw_o.dtype)
        vw_o[gg] = jnp.where(is_pad, 0.0, vwh[gg] + ones_col).astype(vw_o.dtype)


def _kv_pack(z, gks, gkw, batch, seq):
    assert PACK_TS == WIN
    nst = seq // PACK_TS
    cb = Z_KV // LANES + 4

    def zspec(k):
        return pl.BlockSpec((PACK_TS, LANES), lambda b, p, i: (b * nst + jnp.minimum(i, nst - 1), cb + 2 * k + p))

    def ospec(w):
        return pl.BlockSpec((None, 2, PACK_TS, w), lambda b, p, i: (b, p, jnp.minimum(i, nst - 1), 0))

    def wspec(w):
        return pl.BlockSpec((None, 2, PACK_TS, w), lambda b, p, i: (b, p, (i + 1) % (nst + 1), 0))

    def oshape(rows, w):
        return jax.ShapeDtypeStruct((batch, NSA_GROUPS, rows, w), BF16)

    gspec = pl.BlockSpec((1, LANES), lambda b, p, i: (0, 0))
    return pl.pallas_call(
        _kv_pack_kernel,
        out_shape=(oshape(seq, 2 * LANES), oshape(seq, LANES), oshape(seq + WIN, LANES), oshape(seq + WIN, LANES)),
        grid=(batch, NSA_GROUPS // 2, nst + 1),
        in_specs=[zspec(0), zspec(1), zspec(2), zspec(3), gspec, gspec],
        out_specs=(ospec(2 * LANES), ospec(LANES), wspec(LANES), wspec(LANES)),
        compiler_params=_params(("parallel", "arbitrary", "arbitrary")),
        name="kv_pack",
    )(z, z, z, z, gks, gkw)


def _count_outranking(score_ref, cnt_ref, n_live):
    nb, nq = score_ref.shape
    n_chunks = nb // SUBLANES
    cnt_ref[...] = jnp.zeros(cnt_ref.shape, cnt_ref.dtype)
    for ci in range(n_chunks):
        @pl.when(ci < n_live)
        def _():
            sub = lax.broadcasted_iota(jnp.int32, (SUBLANES, nq), 0)
            chunks = [score_ref[SUBLANES * v:SUBLANES * (v + 1), :] for v in range(n_chunks)]
            cnt = [cnt_ref[SUBLANES * v:SUBLANES * (v + 1), :] for v in range(n_chunks)]
            for r in range(SUBLANES):
                row = jnp.broadcast_to(chunks[ci][r:r + 1, :], (SUBLANES, nq))
                for v, c in enumerate(chunks):
                    if v < ci:
                        beats = row > c
                    elif v > ci:
                        beats = row >= c
                    else:
                        beats = ((sub < r) & (row > c)) | ((sub > r) & (row >= c))
                    cnt[v] = cnt[v] + jnp.where(beats, 1, 0)
            for v in range(n_chunks):
                cnt_ref[SUBLANES * v:SUBLANES * (v + 1), :] = cnt[v]


def _nsa_kernel(q_ref, gate_ref, gq_ref, ka_ref, vs_ref, kw_ref, vw_ref, kc_ref, vc_ref, map_ref, band_ref,
                o_ref, m_ref, acc_ref, sa_ref, sb_ref, mta_ref, mtb_ref, qa_ref, score_ref, cnt_ref, oc_ref, ow_ref):
    g = pl.program_id(1)
    t0 = pl.program_id(2) * (NSA_QPB * Q_BLOCK)
    sub = NSA_HPG * Q_BLOCK
    rows = NSA_QPB * sub
    n_q = NSA_QPB * Q_BLOCK
    n_cmp = kc_ref.shape[0]
    scale = NSA_HEAD_DIM ** -0.5 * math.log2(math.e)
    lane = lax.broadcasted_iota(jnp.int32, (1, LANES), 1)

    def block_rows(j):
        return slice(j * sub, (j + 1) * sub)

    local = lax.broadcasted_iota(jnp.int32, (Q_BLOCK, 1), 0)
    t_blocks = [t0 + j * Q_BLOCK + local for j in range(NSA_QPB)]
    t = jnp.concatenate([tb for tb in t_blocks for _ in range(NSA_HPG)], axis=0)
    tq = jnp.concatenate(t_blocks, axis=0)
    cur = jnp.right_shift(tq, SLC_LEN.bit_length() - 1)
    blk = lax.broadcasted_iota(jnp.int32, (1, LANES), 1)

    def head(cols):
        heads = []
        for j in range(NSA_QPB):
            for pair in range(NSA_HPG // 2):
                qn = _half_norm(q_ref[j * Q_BLOCK:(j + 1) * Q_BLOCK, pair * LANES:(pair + 1) * LANES], gq_ref[...])
                heads.extend(_split_halves(qn * scale))
        q = jnp.where(lane == NSA_HEAD_DIM, NEG_INF, jnp.concatenate(heads, axis=0)).astype(BF16)
        qa_ref[:, 0:LANES] = q

        span = WIN + Q_BLOCK
        w_cols = span // LANES
        band = jnp.concatenate([band_ref[...]] * NSA_HPG, axis=0)

        def slab(j, c):
            k = j * w_cols + c
            per = (SEL_TK // LANES) * NSA_QPB
            ref = sa_ref if k < per else sb_ref
            k = k % per
            return ref.at[k // NSA_QPB, pl.ds((k % NSA_QPB) * sub, sub), :]

        for j in range(NSA_QPB):
            start = pl.multiple_of(t0 + j * Q_BLOCK, Q_BLOCK)
            s = _dot_nt(q[block_rows(j)], kw_ref[pl.ds(start, span), :]) + band
            mx = s[:, 0:LANES]
            for c in range(w_cols):
                slab(j, c)[...] = s[:, c * LANES:(c + 1) * LANES]
                if c:
                    mx = jnp.maximum(mx, s[:, c * LANES:(c + 1) * LANES])
            mta_ref[block_rows(j), :] = jnp.broadcast_to(jnp.max(mx, axis=-1, keepdims=True), mx.shape)
        for j in range(NSA_QPB):
            start = pl.multiple_of(t0 + j * Q_BLOCK, Q_BLOCK)
            mt = mta_ref[block_rows(j), :]
            e = jnp.concatenate([jnp.exp2(slab(j, c)[...] - mt) for c in range(w_cols)], axis=1)
            ow_ref[block_rows(j), :] = _dot(e.astype(BF16), vw_ref[pl.ds(start, span), :])

        s = _dot_nt(q, kc_ref[0:cols, :])
        cmp_end = lax.broadcasted_iota(jnp.int32, (1, cols), 1) * CMP_STRIDE + (CMP_LEN - 1)
        s = jnp.where(cmp_end <= t, s, NEG_INF)
        e = jnp.exp2(s - jnp.maximum(jnp.max(s, axis=-1, keepdims=True), 0.1 * NEG_INF))
        p = e / jnp.maximum(jnp.sum(e, axis=-1, keepdims=True), 1e-30)
        oc_ref[...] = _dot(p.astype(BF16), vc_ref[0:cols, :])
        imps = []
        for j in range(NSA_QPB):
            imp = p[j * sub:j * sub + Q_BLOCK]
            for h in range(1, NSA_HPG):
                imp = imp + p[j * sub + h * Q_BLOCK:j * sub + (h + 1) * Q_BLOCK]
            imps.append(imp)
        imp = jnp.concatenate(imps, axis=0)
        hi = imp.astype(BF16)
        r1 = imp - hi.astype(F32)
        mid = r1.astype(BF16)
        lo = (r1 - mid.astype(F32)).astype(BF16)
        wmap = map_ref[0:cols, :]
        imp_s = _dot(hi, wmap) + _dot(mid, wmap) + _dot(lo, wmap)
        forced = (blk == 0) | (blk == cur) | (blk == cur - 1)
        score = jnp.where(forced, FORCE_SCORE, jnp.where(blk <= cur, imp_s, NEG_INF))
        score_ref[...] = score.T

    n_col_blocks = n_cmp // LANES
    live_col_blocks = jnp.clip((t0 + n_q - CMP_LEN) // CMP_STRIDE // LANES + 1, 1, n_col_blocks)
    for n in range(1, n_col_blocks + 1):
        pl.when(live_col_blocks == n)(functools.partial(head, n * LANES))
    _count_outranking(score_ref, cnt_ref, (t0 + n_q - 1) // SLC_LEN // SUBLANES + 1)
    chosen = jnp.where(cnt_ref[...] < SLC_TOP, 0.0, NEG_INF).T
    bias = jnp.where(blk <= cur, chosen, NEG_INF).astype(BF16)
    bias_rows = [bias[j * Q_BLOCK:(j + 1) * Q_BLOCK] for j in range(NSA_QPB) for _ in range(NSA_HPG)]
    qa_ref[:, LANES:2 * LANES] = jnp.concatenate(bias_rows, axis=0)

    n_cols = SEL_TK // LANES
    m_ref[...] = jnp.full(m_ref.shape, NEG_INF, F32)
    acc_ref[...] = jnp.zeros(acc_ref.shape, F32)

    def tile_max(s_ref):
        mx = s_ref[0]
        for c in range(1, n_cols):
            mx = jnp.maximum(mx, s_ref[c])
        return jnp.broadcast_to(jnp.max(mx, axis=-1, keepdims=True), mx.shape)

    def scores(kt, s_ref, mt_ref):
        k0 = pl.multiple_of(kt * SEL_TK, SEL_TK)
        s = _dot_nt(qa_ref[...], ka_ref[pl.ds(k0, SEL_TK), :])
        for c in range(n_cols):
            s_ref[c] = s[:, c * LANES:(c + 1) * LANES]
        mt_ref[...] = tile_max(s_ref)

    def update(kt, s_ref, mt_ref):
        k0 = pl.multiple_of(kt * SEL_TK, SEL_TK)
        m_prev = m_ref[...]
        m_new = jnp.maximum(m_prev, mt_ref[...])
        alpha = jnp.exp2(m_prev - m_new)
        p = jnp.concatenate([jnp.exp2(s_ref[c] - m_new) for c in range(n_cols)], axis=1)
        acc_ref[...] = alpha * acc_ref[...] + _dot(p.astype(BF16), vs_ref[pl.ds(k0, SEL_TK), :])
        m_ref[...] = m_new

    n_full = t0 // SEL_TK
    odd = n_full % 2

    @pl.when(odd == 0)
    def _():
        scores(0, sa_ref, mta_ref)

    @pl.when(odd == 1)
    def _():
        scores(0, sb_ref, mtb_ref)
        scores(1, sa_ref, mta_ref)
        update(0, sb_ref, mtb_ref)

    def pair(j, carry):
        kt = odd + 2 * j
        scores(kt + 1, sb_ref, mtb_ref)
        update(kt, sa_ref, mta_ref)
        scores(kt + 2, sa_ref, mta_ref)
        update(kt + 1, sb_ref, mtb_ref)
        return carry

    lax.fori_loop(0, n_full // 2, pair, 0)
    for j in range(NSA_QPB):
        rs = block_rows(j)
        tj = t0 + j * Q_BLOCK
        dcol = (tj % SEL_TK) // LANES
        kpos = tj + lax.broadcasted_iota(jnp.int32, (1, LANES), 1)
        sa_ref[dcol, rs, :] = jnp.where(kpos <= t[rs], sa_ref[dcol, rs, :], NEG_INF)
    mta_ref[...] = tile_max(sa_ref)
    update(n_full, sa_ref, mta_ref)

    def normalised(acc):
        denom = jnp.sum(jnp.where(lane == NSA_HEAD_DIM, acc, 0.0), axis=-1, keepdims=True)
        return jnp.where(lane < NSA_HEAD_DIM, acc / denom, 0.0)

    for j in range(NSA_QPB):
        qs = slice(j * Q_BLOCK, (j + 1) * Q_BLOCK)
        gates = _sigmoid(gate_ref[qs, :])
        outs = []
        for h in range(NSA_HPG):
            head = g * NSA_HPG + h
            rs = slice(j * sub + h * Q_BLOCK, j * sub + (h + 1) * Q_BLOCK)
            o = jnp.zeros((Q_BLOCK, LANES), F32)
            for r, branch in enumerate((oc_ref[rs, :], normalised(acc_ref[rs, :]), normalised(ow_ref[rs, :]))):
                gcol = jnp.sum(jnp.where(lane == r * NSA_HEADS + head, gates, 0.0), axis=-1, keepdims=True)
                o = o + gcol * branch
            outs.append(o)
        pairs = [outs[2 * k] + pltpu.roll(outs[2 * k + 1], NSA_HEAD_DIM, axis=1) for k in range(NSA_HPG // 2)]
        o_ref[qs, :] = jnp.concatenate(pairs, axis=1).astype(o_ref.dtype)


def _nsa(z, gq, kaug, vsel, kwin, vwin, kvc, wmap, batch, seq):
    n_q = NSA_QPB * Q_BLOCK
    nq = seq // n_q
    gw = NSA_HPG * NSA_HEAD_DIM
    rows = NSA_HPG * n_q
    n_cmp = seq // CMP_STRIDE

    def kvspec(w, pad=0):
        return pl.BlockSpec((None, None, seq + pad, w), lambda b, g, i: (b, g, 0, 0))

    def cspec(s):
        return pl.BlockSpec((None, None, None, n_cmp, LANES), lambda b, g, i: (b, s, g, 0, 0))

    r_idx = np.arange(Q_BLOCK)[:, None]
    c_idx = np.arange(WIN + Q_BLOCK)[None, :]
    band = jnp.asarray(np.where((c_idx > r_idx) & (c_idx <= r_idx + WIN), 0.0, NEG_INF), F32)

    return pl.pallas_call(
        _nsa_kernel,
        out_shape=jax.ShapeDtypeStruct((batch * seq, NSA_HEADS * NSA_HEAD_DIM), BF16),
        grid=(batch, NSA_GROUPS, nq),
        in_specs=[
            pl.BlockSpec((n_q, gw), lambda b, g, i: (b * nq + i, Z_QNSA // gw + g)),
            pl.BlockSpec((n_q, LANES), lambda b, g, i: (b * nq + i, Z_GATE // LANES)),
            pl.BlockSpec((1, LANES), lambda b, g, i: (0, 0)),
            kvspec(2 * LANES), kvspec(LANES), kvspec(LANES, WIN), kvspec(LANES, WIN),
            cspec(0), cspec(1),
            pl.BlockSpec((n_cmp, LANES), lambda b, g, i: (0, 0)),
            pl.BlockSpec((Q_BLOCK, WIN + Q_BLOCK), lambda b, g, i: (0, 0)),
        ],
        out_specs=pl.BlockSpec((n_q, gw), lambda b, g, i: (b * nq + i, g)),
        scratch_shapes=[
            pltpu.VMEM((rows, LANES), F32), pltpu.VMEM((rows, LANES), F32),
            pltpu.VMEM((SEL_TK // LANES, rows, LANES), F32), pltpu.VMEM((SEL_TK // LANES, rows, LANES), F32),
            pltpu.VMEM((rows, LANES), F32), pltpu.VMEM((rows, LANES), F32),
            pltpu.VMEM((rows, 2 * LANES), BF16),
            pltpu.VMEM((LANES, n_q), F32), pltpu.VMEM((LANES, n_q), jnp.int32),
            pltpu.VMEM((rows, LANES), F32), pltpu.VMEM((rows, LANES), F32)],
        compiler_params=_params(("parallel", "parallel", "arbitrary")),
        name="nsa",
    )(z, z, gq, kaug, vsel, kwin, vwin, kvc, kvc, wmap, band)


def _selection_weights(seq):
    n_c = (seq - CMP_LEN) // CMP_STRIDE + 1
    n_s = seq // SLC_LEN
    ratio = SLC_LEN // CMP_STRIDE
    w = np.zeros((seq // CMP_STRIDE, LANES), np.float32)
    for j in range(n_s):
        for off in range(-(CMP_LEN // CMP_STRIDE - 1), ratio):
            ci = ratio * j + off
            if 0 <= ci < n_c:
                cs = ci * CMP_STRIDE
                ov = min(cs + CMP_LEN, (j + 1) * SLC_LEN) - max(cs, j * SLC_LEN)
                w[ci, j] = max(ov, 0) / CMP_LEN
    return jnp.asarray(w, BF16)


def _merge_kernel(h_ref, yc_ref, yn_ref, ym_ref, wgc_ref, wgn_ref, wgm_ref, wb_ref, o_ref):
    h = h_ref[...]
    merged = jnp.zeros(o_ref.shape, F32)
    for n, (y_ref, wg_ref) in enumerate(((yc_ref, wgc_ref), (yn_ref, wgn_ref), (ym_ref, wgm_ref))):
        merged = merged + _sigmoid(_dot(h, wg_ref[...])) * _dot(y_ref[...], wb_ref[n])
    o_ref[...] = merged.astype(o_ref.dtype)


def _merge(h, yc, yn, ym, wg, wb):
    n, d = h.shape
    bw = yc.shape[1]
    nc = d // MERGE_TN

    def gspec(branch):
        return pl.BlockSpec((d, MERGE_TN), lambda i, j: (0, branch * nc + j))

    return pl.pallas_call(
        _merge_kernel,
        out_shape=jax.ShapeDtypeStruct((n, d), BF16),
        grid=(n // MERGE_TM, nc),
        in_specs=[
            pl.BlockSpec((MERGE_TM, d), lambda i, j: (i, 0)),
            pl.BlockSpec((MERGE_TM, bw), lambda i, j: (i, 0)),
            pl.BlockSpec((MERGE_TM, bw), lambda i, j: (i, 0)),
            pl.BlockSpec((MERGE_TM, bw), lambda i, j: (i, 0)),
            gspec(0), gspec(1), gspec(2),
            pl.BlockSpec((N_BRANCH, bw, MERGE_TN), lambda i, j: (0, 0, j)),
        ],
        out_specs=pl.BlockSpec((MERGE_TM, MERGE_TN), lambda i, j: (i, j)),
        compiler_params=_params(("parallel", "arbitrary")),
        name="merge",
    )(h, yc, yn, ym, wg, wg, wg, wb)


def _out_proj_kernel(x_ref, m_ref, w_ref, o_ref):
    o_ref[...] = x_ref[...] + _dot(m_ref[...], w_ref[...])


def _out_proj(x, merged, wo):
    n, d = x.shape
    return pl.pallas_call(
        _out_proj_kernel,
        out_shape=jax.ShapeDtypeStruct((n, d), F32),
        grid=(n // OUT_TM, d // OUT_TN),
        in_specs=[
            pl.BlockSpec((OUT_TM, OUT_TN), lambda i, j: (i, j)),
            pl.BlockSpec((OUT_TM, d), lambda i, j: (i, 0)),
            pl.BlockSpec((d, OUT_TN), lambda i, j: (0, j)),
        ],
        out_specs=pl.BlockSpec((OUT_TM, OUT_TN), lambda i, j: (i, j)),
        compiler_params=_params(("parallel", "arbitrary")),
        name="out_proj",
    )(x, merged, wo)


def _layer(x2d, mem2d, batch, seq, mem_len, p):
    d = x2d.shape[1]
    bw = d // 2
    kvw = NSA_GROUPS * NSA_HEAD_DIM

    w_in = p['w_in'].astype(BF16)
    o_q = 3 * bw
    o_kv = o_q + NSA_HEADS * NSA_HEAD_DIM
    o_g = o_kv + 6 * kvw
    o_qm = o_g + 3 * NSA_HEADS
    o_gm = o_qm + bw
    gate_cols = w_in[:, o_g:o_qm].reshape(d, NSA_HEADS, 3).transpose(0, 2, 1).reshape(d, 3 * NSA_HEADS)
    w_proj = jnp.concatenate([
        w_in[:, :o_q], w_in[:, o_q:o_kv], w_in[:, o_qm:o_gm], w_in[:, o_kv:o_g], gate_cols,
        jnp.zeros((d, Z_COLS - Z_GATE - 3 * NSA_HEADS), BF16)], axis=1)
    w_gm = w_in[:, o_gm:]

    def cmp_weights(w1, w2, pe):
        w1l = w1.reshape(CMP_LEN, NSA_HEAD_DIM, CMP_HIDDEN)
        w1r = jnp.concatenate([w1l[:CMP_STRIDE], w1l[CMP_STRIDE:]], axis=2)
        pe8 = jnp.zeros((CMP_STRIDE, SUBLANES, NSA_HEAD_DIM), F32)
        pe8 = pe8.at[:, 0].set(pe[:CMP_STRIDE]).at[:, 1].set(pe[CMP_STRIDE:])
        w2p = jnp.pad(w2, ((0, 0), (0, LANES - NSA_HEAD_DIM)))
        return w1r.astype(BF16), pe8.astype(BF16), w2p.astype(BF16)

    ck = cmp_weights(p['cmp_w1_k'], p['cmp_w2_k'], p['cmp_pe_k'])
    cv = cmp_weights(p['cmp_w1_v'], p['cmp_w2_v'], p['cmp_pe_v'])
    w1r, pe8, w2p = (jnp.stack([a, b]) for a, b in zip(ck, cv))

    def row(v):
        return v.reshape(1, -1)

    def twice(v):
        return jnp.concatenate([v, v]).reshape(1, LANES)

    gkc = jnp.pad(p['nsa_kc_norm'], (0, LANES - NSA_HEAD_DIM)).reshape(1, LANES)

    x1 = _ffn(x2d, row(p['ffn1_norm']), p['ffn1_w_gate'].astype(BF16), p['ffn1_w_up'].astype(BF16),
              p['ffn1_w_down'].astype(BF16))
    z, h_mix = _in_proj(x1, row(p['mix_norm']), w_proj)
    y_conv = _conv(z, p['conv_w'], batch, seq)
    kvm = _mem_kv(mem2d, row(p['mem_norm']), p['w_mem_kv'].astype(BF16), row(p['mem_k_norm']))
    y_mem = _mem_attn(z, kvm, row(p['mem_q_norm']), batch, seq, mem_len)
    kvc = _compress(z, w1r, pe8, w2p, gkc, batch, seq)
    kaug, vsel, kwin, vwin = _kv_pack(z, twice(p['nsa_ks_norm']), twice(p['nsa_kw_norm']), batch, seq)
    y_nsa = _nsa(z, twice(p['nsa_q_norm']), kaug, vsel, kwin, vwin, kvc, _selection_weights(seq), batch, seq)
    merged = _merge(h_mix, y_conv, y_nsa, y_mem, w_gm, p['w_branch'].astype(BF16))
    x2 = _out_proj(x1, merged, p['w_o'].astype(BF16))
    return _ffn(x2, row(p['ffn2_norm']), p['ffn2_w_gate'].astype(BF16), p['ffn2_w_up'].astype(BF16),
                p['ffn2_w_down'].astype(BF16))


def kernel(x, mem, ffn1_norm, ffn1_w_gate, ffn1_w_up, ffn1_w_down, mix_norm, mem_norm, w_in, conv_w, nsa_q_norm, nsa_kc_norm, nsa_ks_norm, nsa_kw_norm, cmp_pe_k, cmp_w1_k, cmp_w2_k, cmp_pe_v, cmp_w1_v, cmp_w2_v, w_mem_kv, mem_q_norm, mem_k_norm, w_branch, w_o, ffn2_norm, ffn2_w_gate, ffn2_w_up, ffn2_w_down):
    batch, seq, d = x.shape
    mem_len = mem.shape[1]
    assert seq % (SEL_TK * 4) == 0 and seq // SLC_LEN <= LANES and seq >= WIN + Q_BLOCK
    assert SLC_LEN & (SLC_LEN - 1) == 0
    params = dict(
        ffn1_norm=ffn1_norm, ffn1_w_gate=ffn1_w_gate, ffn1_w_up=ffn1_w_up, ffn1_w_down=ffn1_w_down,
        mix_norm=mix_norm, mem_norm=mem_norm, w_in=w_in, conv_w=conv_w, nsa_q_norm=nsa_q_norm,
        nsa_kc_norm=nsa_kc_norm, nsa_ks_norm=nsa_ks_norm, nsa_kw_norm=nsa_kw_norm, cmp_pe_k=cmp_pe_k,
        cmp_w1_k=cmp_w1_k, cmp_w2_k=cmp_w2_k, cmp_pe_v=cmp_pe_v, cmp_w1_v=cmp_w1_v, cmp_w2_v=cmp_w2_v,
        w_mem_kv=w_mem_kv, mem_q_norm=mem_q_norm, mem_k_norm=mem_k_norm, w_branch=w_branch, w_o=w_o,
        ffn2_norm=ffn2_norm, ffn2_w_gate=ffn2_w_gate, ffn2_w_up=ffn2_w_up, ffn2_w_down=ffn2_w_down)
    x2d = x.reshape(batch * seq, d)
    mem2d = mem.reshape(batch * mem_len, d)
    for l in range(ffn1_norm.shape[0]):
        x2d = _layer(x2d, mem2d, batch, seq, mem_len, {k: v[l] for k, v in params.items()})
    return x2d.reshape(batch, seq, d)
```

```python
import functools
import math

import jax
import jax.numpy as jnp
import numpy as np
from jax import lax
from jax.experimental import pallas as pl
from jax.experimental.pallas import tpu as pltpu

F32 = jnp.float32
BF16 = jnp.bfloat16

EPS = 1e-6
NEG_INF = -1e30
FORCE_SCORE = 1e9

CONV_K = 3
NSA_HEAD_DIM = 64
NSA_GROUPS = 4
NSA_HPG = 4
NSA_HEADS = NSA_GROUPS * NSA_HPG
CMP_LEN = 32
CMP_STRIDE = 16
CMP_HIDDEN = 4 * NSA_HEAD_DIM
SLC_LEN = 64
SLC_TOP = 16
WIN = 512
Q_BLOCK = 128
MEM_HEADS = 4
N_BRANCH = 3

LANES = 128
SUBLANES = 8
VMEM_LIMIT_BYTES = 56 * 1024 * 1024

FFN_TM = 512
FFN_TF = 512
PROJ_TM = 1024
PROJ_TN = 768
CONV_TS = 1024
MEM_TM = 1024
PACK_TS = 512
SEL_TK = 512
NSA_QPB = 4
MERGE_TM = 1024
MERGE_TN = 512
OUT_TM = 1024
OUT_TN = 1024

Z_CONV = 0
Z_QNSA = 3072
Z_QMEM = 4096
Z_KV = 5120
Z_GATE = 6656
Z_COLS = 6912


def _params(sem):
    return pltpu.CompilerParams(dimension_semantics=sem, vmem_limit_bytes=VMEM_LIMIT_BYTES)


def _rms(x, g):
    return x * lax.rsqrt(jnp.mean(x * x, axis=-1, keepdims=True) + EPS) * g


def _sigmoid(x):
    return 1.0 / (1.0 + jnp.exp(-x))


def _dot(a, b):
    return jnp.dot(a, b, preferred_element_type=F32)


def _dot_nt(a, b):
    return lax.dot_general(a, b, (((1,), (1,)), ((), ())), preferred_element_type=F32)


def _half_norm(x, g):
    lane = lax.broadcasted_iota(jnp.int32, x.shape, 1)
    lo = lane < NSA_HEAD_DIM
    xx = x * x
    s_lo = jnp.sum(jnp.where(lo, xx, 0.0), axis=-1, keepdims=True)
    s_hi = jnp.sum(jnp.where(lo, 0.0, xx), axis=-1, keepdims=True)
    r = jnp.where(lo, lax.rsqrt(s_lo / NSA_HEAD_DIM + EPS), lax.rsqrt(s_hi / NSA_HEAD_DIM + EPS))
    return x * r * g


def _split_halves(x):
    lane = lax.broadcasted_iota(jnp.int32, x.shape, 1)
    lo = lane < NSA_HEAD_DIM
    return jnp.where(lo, x, 0.0), jnp.where(lo, pltpu.roll(x, NSA_HEAD_DIM, axis=1), 0.0)


def _ffn_kernel(x_ref, g_ref, wg_ref, wu_ref, wd_ref, o_ref, h_ref):
    j = pl.program_id(1)

    @pl.when(j == 0)
    def _():
        h_ref[...] = _rms(x_ref[...], g_ref[...]).astype(BF16)
        o_ref[...] = jnp.zeros_like(o_ref)

    h = h_ref[...]
    gate = _dot(h, wg_ref[...])
    up = _dot(h, wu_ref[...])
    act = (gate * _sigmoid(gate) * up).astype(BF16)
    o_ref[...] += _dot(act, wd_ref[...])

    @pl.when(j == pl.num_programs(1) - 1)
    def _():
        o_ref[...] = x_ref[...] + 0.5 * o_ref[...]


def _ffn(x, g, wg, wu, wd):
    n, d = x.shape
    f = wg.shape[1]
    return pl.pallas_call(
        _ffn_kernel,
        out_shape=jax.ShapeDtypeStruct((n, d), F32),
        grid=(n // FFN_TM, f // FFN_TF),
        in_specs=[
            pl.BlockSpec((FFN_TM, d), lambda i, j: (i, 0)),
            pl.BlockSpec((1, d), lambda i, j: (0, 0)),
            pl.BlockSpec((d, FFN_TF), lambda i, j: (0, j)),
            pl.BlockSpec((d, FFN_TF), lambda i, j: (0, j)),
            pl.BlockSpec((FFN_TF, d), lambda i, j: (j, 0)),
        ],
        out_specs=pl.BlockSpec((FFN_TM, d), lambda i, j: (i, 0)),
        scratch_shapes=[pltpu.VMEM((FFN_TM, d), BF16)],
        compiler_params=_params(("parallel", "arbitrary")),
        name="ffn",
    )(x, g, wg, wu, wd)


def _proj_kernel(x_ref, g_ref, w_ref, o_ref, h_ref):
    @pl.when(pl.program_id(1) == 0)
    def _():
        h_ref[...] = _rms(x_ref[...], g_ref[...]).astype(BF16)

    o_ref[...] = _dot(h_ref[...], w_ref[...])


def _in_proj(x, g, w):
    n, d = x.shape
    c = w.shape[1]
    return pl.pallas_call(
        _proj_kernel,
        out_shape=(jax.ShapeDtypeStruct((n, c), F32), jax.ShapeDtypeStruct((n, d), BF16)),
        grid=(n // PROJ_TM, c // PROJ_TN),
        in_specs=[
            pl.BlockSpec((PROJ_TM, d), lambda i, j: (i, 0)),
            pl.BlockSpec((1, d), lambda i, j: (0, 0)),
            pl.BlockSpec((d, PROJ_TN), lambda i, j: (0, j)),
        ],
        out_specs=(pl.BlockSpec((PROJ_TM, PROJ_TN), lambda i, j: (i, j)),
                   pl.BlockSpec((PROJ_TM, d), lambda i, j: (i, 0))),
        compiler_params=_params(("parallel", "arbitrary")),
        name="in_proj",
    )(x, g, w)


def _conv_kernel(b_ref, c_ref, u_ref, cp_ref, up_ref, w_ref, o_ref, z_ref):
    ts = b_ref.shape[0]
    first = pl.program_id(1) == 0
    zp = cp_ref[...] * up_ref[...]
    z_ref[pl.ds(0, SUBLANES), :] = jnp.where(first, 0.0, zp)
    z = c_ref[...] * u_ref[...]
    z_ref[pl.ds(SUBLANES, ts), :] = z
    z1 = z_ref[pl.ds(SUBLANES - 1, ts), :]
    z2 = z_ref[pl.ds(SUBLANES - 2, ts), :]
    w = w_ref[...]
    y = b_ref[...] * (w[0:1, :] * z2 + w[1:2, :] * z1 + w[2:3, :] * z)
    o_ref[...] = y.astype(o_ref.dtype)


def _conv(z, conv_w, batch, seq):
    cw = conv_w.shape[1]
    nst = seq // CONV_TS
    rows_per = CONV_TS // SUBLANES
    cb = Z_CONV // cw

    def prev_map(col):
        return lambda b, i: (jnp.maximum((b * nst + i) * rows_per - 1, 0), col)

    return pl.pallas_call(
        _conv_kernel,
        out_shape=jax.ShapeDtypeStruct((batch * seq, cw), BF16),
        grid=(batch, nst),
        in_specs=[
            pl.BlockSpec((CONV_TS, cw), lambda b, i: (b * nst + i, cb)),
            pl.BlockSpec((CONV_TS, cw), lambda b, i: (b * nst + i, cb + 1)),
            pl.BlockSpec((CONV_TS, cw), lambda b, i: (b * nst + i, cb + 2)),
            pl.BlockSpec((SUBLANES, cw), prev_map(cb + 1)),
            pl.BlockSpec((SUBLANES, cw), prev_map(cb + 2)),
            pl.BlockSpec((CONV_K, cw), lambda b, i: (0, 0)),
        ],
        out_specs=pl.BlockSpec((CONV_TS, cw), lambda b, i: (b * nst + i, 0)),
        scratch_shapes=[pltpu.VMEM((CONV_TS + SUBLANES, cw), F32)],
        compiler_params=_params(("parallel", "arbitrary")),
        name="conv",
    )(z, z, z, z, z, conv_w)


def _mem_kv_kernel(m_ref, g_ref, w_ref, gk_ref, o_ref, *, n_key_tiles):
    h = _rms(m_ref[...], g_ref[...]).astype(BF16)
    kv = _dot(h, w_ref[...])
    is_key = pl.program_id(0) < n_key_tiles
    o_ref[...] = jnp.where(is_key, _rms(kv, gk_ref[...]), kv).astype(o_ref.dtype)


def _mem_kv(mem2d, g, w, gk):
    rows, d = mem2d.shape
    hd = gk.shape[1]
    cols = w.shape[1]
    return pl.pallas_call(
        functools.partial(_mem_kv_kernel, n_key_tiles=cols // (2 * hd)),
        out_shape=jax.ShapeDtypeStruct((rows, cols), BF16),
        grid=(cols // hd,),
        in_specs=[
            pl.BlockSpec((rows, d), lambda j: (0, 0)),
            pl.BlockSpec((1, d), lambda j: (0, 0)),
            pl.BlockSpec((d, hd), lambda j: (0, j)),
            pl.BlockSpec((1, hd), lambda j: (0, 0)),
        ],
        out_specs=pl.BlockSpec((rows, hd), lambda j: (0, j)),
        compiler_params=_params(("arbitrary",)),
        name="mem_kv",
    )(mem2d, g, w, gk)


def _mem_attn_kernel(q_ref, k_ref, v_ref, gq_ref, o_ref):
    hd = gq_ref.shape[1]
    scale = hd ** -0.5
    outs = []
    for h in range(MEM_HEADS):
        sl = slice(h * hd, (h + 1) * hd)
        q = (_rms(q_ref[:, sl], gq_ref[...]) * scale).astype(BF16)
        s = _dot_nt(q, k_ref[:, sl])
        e = jnp.exp(s - jnp.max(s, axis=-1, keepdims=True))
        p = e / jnp.sum(e, axis=-1, keepdims=True)
        outs.append(_dot(p.astype(BF16), v_ref[:, sl]))
    o_ref[...] = jnp.concatenate(outs, axis=1).astype(o_ref.dtype)


def _mem_attn(z, kvm, gq, batch, seq, mem_len):
    width = MEM_HEADS * gq.shape[1]
    nt = seq // MEM_TM
    return pl.pallas_call(
        _mem_attn_kernel,
        out_shape=jax.ShapeDtypeStruct((batch * seq, width), BF16),
        grid=(batch, nt),
        in_specs=[
            pl.BlockSpec((MEM_TM, width), lambda b, i: (b * nt + i, Z_QMEM // width)),
            pl.BlockSpec((mem_len, width), lambda b, i: (b, 0)),
            pl.BlockSpec((mem_len, width), lambda b, i: (b, 1)),
            pl.BlockSpec((1, gq.shape[1]), lambda b, i: (0, 0)),
        ],
        out_specs=pl.BlockSpec((MEM_TM, width), lambda b, i: (b * nt + i, 0)),
        compiler_params=_params(("parallel", "arbitrary")),
        name="mem_attn",
    )(z, kvm, kvm, gq)


def _compress_kernel(x_ref, w1_ref, pe_ref, w2_ref, g_ref, o_ref, b_ref):
    rows = o_ref.shape[1]
    is_key = pl.program_id(1) == 0
    ab = [jnp.zeros((rows, 2 * CMP_HIDDEN), F32) for _ in range(2)]
    peb = jnp.zeros((SUBLANES, 2 * CMP_HIDDEN), F32)
    for l in range(CMP_STRIDE):
        x = x_ref[pl.ds(l, rows, stride=CMP_STRIDE), :]
        w = w1_ref[l]
        halves = _split_halves(x)
        for gg in range(2):
            ab[gg] = ab[gg] + _dot(halves[gg][:, :NSA_HEAD_DIM].astype(BF16), w)
        peb = peb + _dot(pe_ref[l], w)
    bias = peb[0:1, :CMP_HIDDEN] + peb[1:2, CMP_HIDDEN:]
    b_ref[pl.ds(rows, SUBLANES), :] = jnp.zeros((SUBLANES, CMP_HIDDEN), F32)
    for gg in range(2):
        b_ref[pl.ds(0, rows), :] = ab[gg][:, CMP_HIDDEN:]
        hid = ab[gg][:, :CMP_HIDDEN] + b_ref[pl.ds(1, rows), :] + bias
        act = (hid * _sigmoid(hid)).astype(BF16)
        out = _dot(act, w2_ref[...])
        ms = jnp.sum(out * out, axis=-1, keepdims=True) / NSA_HEAD_DIM
        normed = out * lax.rsqrt(ms + EPS) * g_ref[...]
        o_ref[gg] = jnp.where(is_key, normed, out).astype(o_ref.dtype)


def _compress(z, w1r, pe8, w2p, gkc, batch, seq):
    rows = seq // CMP_STRIDE
    cb = Z_KV // LANES
    return pl.pallas_call(
        _compress_kernel,
        out_shape=jax.ShapeDtypeStruct((batch, 2, NSA_GROUPS, rows, LANES), BF16),
        grid=(batch, 2, NSA_GROUPS // 2),
        in_specs=[
            pl.BlockSpec((seq, LANES), lambda b, s, p: (b, cb + 2 * s + p)),
            pl.BlockSpec((None, CMP_STRIDE, NSA_HEAD_DIM, 2 * CMP_HIDDEN), lambda b, s, p: (s, 0, 0, 0)),
            pl.BlockSpec((None, CMP_STRIDE, SUBLANES, NSA_HEAD_DIM), lambda b, s, p: (s, 0, 0, 0)),
            pl.BlockSpec((None, CMP_HIDDEN, LANES), lambda b, s, p: (s, 0, 0)),
            pl.BlockSpec((1, LANES), lambda b, s, p: (0, 0)),
        ],
        out_specs=pl.BlockSpec((None, None, 2, rows, LANES), lambda b, s, p: (b, s, p, 0, 0)),
        scratch_shapes=[pltpu.VMEM((rows + SUBLANES, CMP_HIDDEN), F32)],
        compiler_params=_params(("parallel", "arbitrary", "arbitrary")),
        name="compress",
    )(z, w1r, pe8, w2p, gkc)


def _kv_pack_kernel(ks_ref, vs_ref, kw_ref, vw_ref, gks_ref, gkw_ref, ka_ref, vs_o, kw_o, vw_o):
    ts = ks_ref.shape[0]
    n_tiles = pl.num_programs(2) - 1
    i = pl.program_id(2)
    is_pad = i == n_tiles
    ksn = _split_halves(_half_norm(ks_ref[...], gks_ref[...]))
    kwn = _split_halves(_half_norm(kw_ref[...], gkw_ref[...]))
    vsh = _split_halves(vs_ref[...])
    vwh = _split_halves(vw_ref[...])
    pos = jnp.minimum(i, n_tiles - 1) * ts + lax.broadcasted_iota(jnp.int32, (ts, LANES), 0)
    blk = lax.broadcasted_iota(jnp.int32, (ts, LANES), 1)
    onehot = jnp.where(jnp.right_shift(pos, SLC_LEN.bit_length() - 1) == blk, 1.0, 0.0)
    ones_col = jnp.where(blk == NSA_HEAD_DIM, 1.0, 0.0)
    for gg in range(2):
        ka_ref[gg] = jnp.concatenate([ksn[gg], onehot], axis=1).astype(ka_ref.dtype)
        vs_o[gg] = (vsh[gg] + ones_col).astype(vs_o.dtype)
        kw_o[gg] = jnp.where(is_pad, ones_col, kwn[gg]).astype(kw_o.dtype)
        vw_o[gg] = jnp.where(is_pad, 0.0, vwh[gg] + ones_col).astype(vw_o.dtype)


def _kv_pack(z, gks, gkw, batch, seq):
    assert PACK_TS == WIN
    nst = seq // PACK_TS
    cb = Z_KV // LANES + 4

    def zspec(k):
        return pl.BlockSpec((PACK_TS, LANES), lambda b, p, i: (b * nst + jnp.minimum(i, nst - 1), cb + 2 * k + p))

    def ospec(w):
        return pl.BlockSpec((None, 2, PACK_TS, w), lambda b, p, i: (b, p, jnp.minimum(i, nst - 1), 0))

    def wspec(w):
        return pl.BlockSpec((None, 2, PACK_TS, w), lambda b, p, i: (b, p, (i + 1) % (nst + 1), 0))

    def oshape(rows, w):
        return jax.ShapeDtypeStruct((batch, NSA_GROUPS, rows, w), BF16)

    gspec = pl.BlockSpec((1, LANES), lambda b, p, i: (0, 0))
    return pl.pallas_call(
        _kv_pack_kernel,
        out_shape=(oshape(seq, 2 * LANES), oshape(seq, LANES), oshape(seq + WIN, LANES), oshape(seq + WIN, LANES)),
        grid=(batch, NSA_GROUPS // 2, nst + 1),
        in_specs=[zspec(0), zspec(1), zspec(2), zspec(3), gspec, gspec],
        out_specs=(ospec(2 * LANES), ospec(LANES), wspec(LANES), wspec(LANES)),
        compiler_params=_params(("parallel", "arbitrary", "arbitrary")),
        name="kv_pack",
    )(z, z, z, z, gks, gkw)


def _count_outranking(score_ref, cnt_ref, n_live):
    nb, nq = score_ref.shape
    n_chunks = nb // SUBLANES
    cnt_ref[...] = jnp.zeros(cnt_ref.shape, cnt_ref.dtype)
    for ci in range(n_chunks):
        @pl.when(ci < n_live)
        def _():
            sub = lax.broadcasted_iota(jnp.int32, (SUBLANES, nq), 0)
            chunks = [score_ref[SUBLANES * v:SUBLANES * (v + 1), :] for v in range(n_chunks)]
            cnt = [cnt_ref[SUBLANES * v:SUBLANES * (v + 1), :] for v in range(n_chunks)]
            for r in range(SUBLANES):
                row = jnp.broadcast_to(chunks[ci][r:r + 1, :], (SUBLANES, nq))
                for v, c in enumerate(chunks):
                    if v < ci:
                        beats = row > c
                    elif v > ci:
                        beats = row >= c
                    else:
                        beats = ((sub < r) & (row > c)) | ((sub > r) & (row >= c))
                    cnt[v] = cnt[v] + jnp.where(beats, 1, 0)
            for v in range(n_chunks):
                cnt_ref[SUBLANES * v:SUBLANES * (v + 1), :] = cnt[v]


def _nsa_kernel(q_ref, gate_ref, gq_ref, ka_ref, vs_ref, kw_ref, vw_ref, kc_ref, vc_ref, map_ref, band_ref,
                o_ref, m_ref, acc_ref, sa_ref, sb_ref, mta_ref, mtb_ref, qa_ref, score_ref, cnt_ref, og_ref):
    g = pl.program_id(1)
    t0 = pl.program_id(2) * (NSA_QPB * Q_BLOCK)
    sub = NSA_HPG * Q_BLOCK
    rows = NSA_QPB * sub
    n_q = NSA_QPB * Q_BLOCK
    n_cmp = kc_ref.shape[0]
    scale = NSA_HEAD_DIM ** -0.5 * math.log2(math.e)
    lane = lax.broadcasted_iota(jnp.int32, (1, LANES), 1)

    def block_rows(j):
        return slice(j * sub, (j + 1) * sub)

    def gate_col(gates, r, h):
        return jnp.sum(jnp.where(lane == r * NSA_HEADS + g * NSA_HPG + h, gates, 0.0), axis=-1, keepdims=True)

    def normalised(acc):
        denom = jnp.sum(jnp.where(lane == NSA_HEAD_DIM, acc, 0.0), axis=-1, keepdims=True)
        return jnp.where(lane < NSA_HEAD_DIM, acc / denom, 0.0)

    local = lax.broadcasted_iota(jnp.int32, (Q_BLOCK, 1), 0)
    t_blocks = [t0 + j * Q_BLOCK + local for j in range(NSA_QPB)]
    t = jnp.concatenate([tb for tb in t_blocks for _ in range(NSA_HPG)], axis=0)
    tq = jnp.concatenate(t_blocks, axis=0)
    cur = jnp.right_shift(tq, SLC_LEN.bit_length() - 1)
    blk = lax.broadcasted_iota(jnp.int32, (1, LANES), 1)

    def head(cols):
        heads = []
        for j in range(NSA_QPB):
            for pair in range(NSA_HPG // 2):
                qn = _half_norm(q_ref[j * Q_BLOCK:(j + 1) * Q_BLOCK, pair * LANES:(pair + 1) * LANES], gq_ref[...])
                heads.extend(_split_halves(qn * scale))
        q = jnp.where(lane == NSA_HEAD_DIM, NEG_INF, jnp.concatenate(heads, axis=0)).astype(BF16)
        qa_ref[:, 0:LANES] = q

        span = WIN + Q_BLOCK
        w_cols = span // LANES
        band = jnp.concatenate([band_ref[...]] * NSA_HPG, axis=0)

        def slab(j, c):
            k = j * w_cols + c
            per = (SEL_TK // LANES) * NSA_QPB
            ref = sa_ref if k < per else sb_ref
            k = k % per
            return ref.at[k // NSA_QPB, pl.ds((k % NSA_QPB) * sub, sub), :]

        for j in range(NSA_QPB):
            start = pl.multiple_of(t0 + j * Q_BLOCK, Q_BLOCK)
            s = _dot_nt(q[block_rows(j)], kw_ref[pl.ds(start, span), :]) + band
            mx = s[:, 0:LANES]
            for c in range(w_cols):
                slab(j, c)[...] = s[:, c * LANES:(c + 1) * LANES]
                if c:
                    mx = jnp.maximum(mx, s[:, c * LANES:(c + 1) * LANES])
            mta_ref[block_rows(j), :] = jnp.broadcast_to(jnp.max(mx, axis=-1, keepdims=True), mx.shape)
        o_win = []
        for j in range(NSA_QPB):
            start = pl.multiple_of(t0 + j * Q_BLOCK, Q_BLOCK)
            mt = mta_ref[block_rows(j), :]
            e = jnp.concatenate([jnp.exp2(slab(j, c)[...] - mt) for c in range(w_cols)], axis=1)
            o_win.append(_dot(e.astype(BF16), vw_ref[pl.ds(start, span), :]))

        s = _dot_nt(q, kc_ref[0:cols, :])
        cmp_end = lax.broadcasted_iota(jnp.int32, (1, cols), 1) * CMP_STRIDE + (CMP_LEN - 1)
        s = jnp.where(cmp_end <= t, s, NEG_INF)
        e = jnp.exp2(s - jnp.maximum(jnp.max(s, axis=-1, keepdims=True), 0.1 * NEG_INF))
        p = e / jnp.maximum(jnp.sum(e, axis=-1, keepdims=True), 1e-30)
        o_cmp = _dot(p.astype(BF16), vc_ref[0:cols, :])
        for j in range(NSA_QPB):
            gates = _sigmoid(gate_ref[j * Q_BLOCK:(j + 1) * Q_BLOCK, :])
            for h in range(NSA_HPG):
                hs = slice(h * Q_BLOCK, (h + 1) * Q_BLOCK)
                rs = slice(j * sub + h * Q_BLOCK, j * sub + (h + 1) * Q_BLOCK)
                og_ref[rs, :] = (gate_col(gates, 0, h) * o_cmp[rs]
                                 + gate_col(gates, 2, h) * normalised(o_win[j][hs]))
        imps = []
        for j in range(NSA_QPB):
            imp = p[j * sub:j * sub + Q_BLOCK]
            for h in range(1, NSA_HPG):
                imp = imp + p[j * sub + h * Q_BLOCK:j * sub + (h + 1) * Q_BLOCK]
            imps.append(imp)
        imp = jnp.concatenate(imps, axis=0)
        hi = imp.astype(BF16)
        r1 = imp - hi.astype(F32)
        mid = r1.astype(BF16)
        lo = (r1 - mid.astype(F32)).astype(BF16)
        wmap = map_ref[0:cols, :]
        imp_s = _dot(hi, wmap) + _dot(mid, wmap) + _dot(lo, wmap)
        forced = (blk == 0) | (blk == cur) | (blk == cur - 1)
        score = jnp.where(forced, FORCE_SCORE, jnp.where(blk <= cur, imp_s, NEG_INF))
        score_ref[...] = score.T

    n_col_blocks = n_cmp // LANES
    live_col_blocks = jnp.clip((t0 + n_q - CMP_LEN) // CMP_STRIDE // LANES + 1, 1, n_col_blocks)
    for n in range(1, n_col_blocks + 1):
        pl.when(live_col_blocks == n)(functools.partial(head, n * LANES))
    _count_outranking(score_ref, cnt_ref, (t0 + n_q - 1) // SLC_LEN // SUBLANES + 1)
    chosen = jnp.where(cnt_ref[...] < SLC_TOP, 0.0, NEG_INF).T
    bias = jnp.where(blk <= cur, chosen, NEG_INF).astype(BF16)
    bias_rows = [bias[j * Q_BLOCK:(j + 1) * Q_BLOCK] for j in range(NSA_QPB) for _ in range(NSA_HPG)]
    qa_ref[:, LANES:2 * LANES] = jnp.concatenate(bias_rows, axis=0)

    n_cols = SEL_TK // LANES
    m_ref[...] = jnp.full(m_ref.shape, NEG_INF, F32)
    acc_ref[...] = jnp.zeros(acc_ref.shape, F32)

    def tile_max(s_ref):
        mx = s_ref[0]
        for c in range(1, n_cols):
            mx = jnp.maximum(mx, s_ref[c])
        return jnp.broadcast_to(jnp.max(mx, axis=-1, keepdims=True), mx.shape)

    def scores(kt, s_ref, mt_ref):
        k0 = pl.multiple_of(kt * SEL_TK, SEL_TK)
        s = _dot_nt(qa_ref[...], ka_ref[pl.ds(k0, SEL_TK), :])
        for c in range(n_cols):
            s_ref[c] = s[:, c * LANES:(c + 1) * LANES]
        mt_ref[...] = tile_max(s_ref)

    def update(kt, s_ref, mt_ref):
        k0 = pl.multiple_of(kt * SEL_TK, SEL_TK)
        m_prev = m_ref[...]
        m_new = jnp.maximum(m_prev, mt_ref[...])
        alpha = jnp.exp2(m_prev - m_new)
        p = jnp.concatenate([jnp.exp2(s_ref[c] - m_new) for c in range(n_cols)], axis=1)
        acc_ref[...] = alpha * acc_ref[...] + _dot(p.astype(BF16), vs_ref[pl.ds(k0, SEL_TK), :])
        m_ref[...] = m_new

    n_full = t0 // SEL_TK
    odd = n_full % 2

    @pl.when(odd == 0)
    def _():
        scores(0, sa_ref, mta_ref)

    @pl.when(odd == 1)
    def _():
        scores(0, sb_ref, mtb_ref)
        scores(1, sa_ref, mta_ref)
        update(0, sb_ref, mtb_ref)

    def pair(j, carry):
        kt = odd + 2 * j
        scores(kt + 1, sb_ref, mtb_ref)
        update(kt, sa_ref, mta_ref)
        scores(kt + 2, sa_ref, mta_ref)
        update(kt + 1, sb_ref, mtb_ref)
        return carry

    lax.fori_loop(0, n_full // 2, pair, 0)
    for j in range(NSA_QPB):
        rs = block_rows(j)
        tj = t0 + j * Q_BLOCK
        dcol = (tj % SEL_TK) // LANES
        kpos = tj + lax.broadcasted_iota(jnp.int32, (1, LANES), 1)
        sa_ref[dcol, rs, :] = jnp.where(kpos <= t[rs], sa_ref[dcol, rs, :], NEG_INF)
    mta_ref[...] = tile_max(sa_ref)
    update(n_full, sa_ref, mta_ref)

    for j in range(NSA_QPB):
        qs = slice(j * Q_BLOCK, (j + 1) * Q_BLOCK)
        gates = _sigmoid(gate_ref[qs, :])
        outs = []
        for h in range(NSA_HPG):
            rs = slice(j * sub + h * Q_BLOCK, j * sub + (h + 1) * Q_BLOCK)
            outs.append(og_ref[rs, :] + gate_col(gates, 1, h) * normalised(acc_ref[rs, :]))
        pairs = [outs[2 * k] + pltpu.roll(outs[2 * k + 1], NSA_HEAD_DIM, axis=1) for k in range(NSA_HPG // 2)]
        o_ref[qs, :] = jnp.concatenate(pairs, axis=1).astype(o_ref.dtype)


def _nsa(z, gq, kaug, vsel, kwin, vwin, kvc, wmap, batch, seq):
    n_q = NSA_QPB * Q_BLOCK
    nq = seq // n_q
    gw = NSA_HPG * NSA_HEAD_DIM
    rows = NSA_HPG * n_q
    n_cmp = seq // CMP_STRIDE

    def kvspec(w, pad=0):
        return pl.BlockSpec((None, None, seq + pad, w), lambda b, g, i: (b, g, 0, 0))

    def cspec(s):
        return pl.BlockSpec((None, None, None, n_cmp, LANES), lambda b, g, i: (b, s, g, 0, 0))

    r_idx = np.arange(Q_BLOCK)[:, None]
    c_idx = np.arange(WIN + Q_BLOCK)[None, :]
    band = jnp.asarray(np.where((c_idx > r_idx) & (c_idx <= r_idx + WIN), 0.0, NEG_INF), F32)

    return pl.pallas_call(
        _nsa_kernel,
        out_shape=jax.ShapeDtypeStruct((batch * seq, NSA_HEADS * NSA_HEAD_DIM), BF16),
        grid=(batch, NSA_GROUPS, nq),
        in_specs=[
            pl.BlockSpec((n_q, gw), lambda b, g, i: (b * nq + i, Z_QNSA // gw + g)),
            pl.BlockSpec((n_q, LANES), lambda b, g, i: (b * nq + i, Z_GATE // LANES)),
            pl.BlockSpec((1, LANES), lambda b, g, i: (0, 0)),
            kvspec(2 * LANES), kvspec(LANES), kvspec(LANES, WIN), kvspec(LANES, WIN),
            cspec(0), cspec(1),
            pl.BlockSpec((n_cmp, LANES), lambda b, g, i: (0, 0)),
            pl.BlockSpec((Q_BLOCK, WIN + Q_BLOCK), lambda b, g, i: (0, 0)),
        ],
        out_specs=pl.BlockSpec((n_q, gw), lambda b, g, i: (b * nq + i, g)),
        scratch_shapes=[
            pltpu.VMEM((rows, LANES), F32), pltpu.VMEM((rows, LANES), F32),
            pltpu.VMEM((SEL_TK // LANES, rows, LANES), F32), pltpu.VMEM((SEL_TK // LANES, rows, LANES), F32),
            pltpu.VMEM((rows, LANES), F32), pltpu.VMEM((rows, LANES), F32),
            pltpu.VMEM((rows, 2 * LANES), BF16),
            pltpu.VMEM((LANES, n_q), F32), pltpu.VMEM((LANES, n_q), jnp.int32),
            pltpu.VMEM((rows, LANES), F32)],
        compiler_params=_params(("parallel", "parallel", "arbitrary")),
        name="nsa",
    )(z, z, gq, kaug, vsel, kwin, vwin, kvc, kvc, wmap, band)


def _selection_weights(seq):
    n_c = (seq - CMP_LEN) // CMP_STRIDE + 1
    n_s = seq // SLC_LEN
    ratio = SLC_LEN // CMP_STRIDE
    w = np.zeros((seq // CMP_STRIDE, LANES), np.float32)
    for j in range(n_s):
        for off in range(-(CMP_LEN // CMP_STRIDE - 1), ratio):
            ci = ratio * j + off
            if 0 <= ci < n_c:
                cs = ci * CMP_STRIDE
                ov = min(cs + CMP_LEN, (j + 1) * SLC_LEN) - max(cs, j * SLC_LEN)
                w[ci, j] = max(ov, 0) / CMP_LEN
    return jnp.asarray(w, BF16)


def _merge_kernel(h_ref, yc_ref, yn_ref, ym_ref, wgc_ref, wgn_ref, wgm_ref, wb_ref, o_ref):
    h = h_ref[...]
    merged = jnp.zeros(o_ref.shape, F32)
    for n, (y_ref, wg_ref) in enumerate(((yc_ref, wgc_ref), (yn_ref, wgn_ref), (ym_ref, wgm_ref))):
        merged = merged + _sigmoid(_dot(h, wg_ref[...])) * _dot(y_ref[...], wb_ref[n])
    o_ref[...] = merged.astype(o_ref.dtype)


def _merge(h, yc, yn, ym, wg, wb):
    n, d = h.shape
    bw = yc.shape[1]
    nc = d // MERGE_TN

    def gspec(branch):
        return pl.BlockSpec((d, MERGE_TN), lambda i, j: (0, branch * nc + j))

    return pl.pallas_call(
        _merge_kernel,
        out_shape=jax.ShapeDtypeStruct((n, d), BF16),
        grid=(n // MERGE_TM, nc),
        in_specs=[
            pl.BlockSpec((MERGE_TM, d), lambda i, j: (i, 0)),
            pl.BlockSpec((MERGE_TM, bw), lambda i, j: (i, 0)),
            pl.BlockSpec((MERGE_TM, bw), lambda i, j: (i, 0)),
            pl.BlockSpec((MERGE_TM, bw), lambda i, j: (i, 0)),
            gspec(0), gspec(1), gspec(2),
            pl.BlockSpec((N_BRANCH, bw, MERGE_TN), lambda i, j: (0, 0, j)),
        ],
        out_specs=pl.BlockSpec((MERGE_TM, MERGE_TN), lambda i, j: (i, j)),
        compiler_params=_params(("parallel", "arbitrary")),
        name="merge",
    )(h, yc, yn, ym, wg, wg, wg, wb)


def _out_proj_kernel(x_ref, m_ref, w_ref, o_ref):
    o_ref[...] = x_ref[...] + _dot(m_ref[...], w_ref[...])


def _out_proj(x, merged, wo):
    n, d = x.shape
    return pl.pallas_call(
        _out_proj_kernel,
        out_shape=jax.ShapeDtypeStruct((n, d), F32),
        grid=(n // OUT_TM, d // OUT_TN),
        in_specs=[
            pl.BlockSpec((OUT_TM, OUT_TN), lambda i, j: (i, j)),
            pl.BlockSpec((OUT_TM, d), lambda i, j: (i, 0)),
            pl.BlockSpec((d, OUT_TN), lambda i, j: (0, j)),
        ],
        out_specs=pl.BlockSpec((OUT_TM, OUT_TN), lambda i, j: (i, j)),
        compiler_params=_params(("parallel", "arbitrary")),
        name="out_proj",
    )(x, merged, wo)


def _layer(x2d, mem2d, batch, seq, mem_len, p):
    d = x2d.shape[1]
    bw = d // 2
    kvw = NSA_GROUPS * NSA_HEAD_DIM

    w_in = p['w_in'].astype(BF16)
    o_q = 3 * bw
    o_kv = o_q + NSA_HEADS * NSA_HEAD_DIM
    o_g = o_kv + 6 * kvw
    o_qm = o_g + 3 * NSA_HEADS
    o_gm = o_qm + bw
    gate_cols = w_in[:, o_g:o_qm].reshape(d, NSA_HEADS, 3).transpose(0, 2, 1).reshape(d, 3 * NSA_HEADS)
    w_proj = jnp.concatenate([
        w_in[:, :o_q], w_in[:, o_q:o_kv], w_in[:, o_qm:o_gm], w_in[:, o_kv:o_g], gate_cols,
        jnp.zeros((d, Z_COLS - Z_GATE - 3 * NSA_HEADS), BF16)], axis=1)
    w_gm = w_in[:, o_gm:]

    def cmp_weights(w1, w2, pe):
        w1l = w1.reshape(CMP_LEN, NSA_HEAD_DIM, CMP_HIDDEN)
        w1r = jnp.concatenate([w1l[:CMP_STRIDE], w1l[CMP_STRIDE:]], axis=2)
        pe8 = jnp.zeros((CMP_STRIDE, SUBLANES, NSA_HEAD_DIM), F32)
        pe8 = pe8.at[:, 0].set(pe[:CMP_STRIDE]).at[:, 1].set(pe[CMP_STRIDE:])
        w2p = jnp.pad(w2, ((0, 0), (0, LANES - NSA_HEAD_DIM)))
        return w1r.astype(BF16), pe8.astype(BF16), w2p.astype(BF16)

    ck = cmp_weights(p['cmp_w1_k'], p['cmp_w2_k'], p['cmp_pe_k'])
    cv = cmp_weights(p['cmp_w1_v'], p['cmp_w2_v'], p['cmp_pe_v'])
    w1r, pe8, w2p = (jnp.stack([a, b]) for a, b in zip(ck, cv))

    def row(v):
        return v.reshape(1, -1)

    def twice(v):
        return jnp.concatenate([v, v]).reshape(1, LANES)

    gkc = jnp.pad(p['nsa_kc_norm'], (0, LANES - NSA_HEAD_DIM)).reshape(1, LANES)

    x1 = _ffn(x2d, row(p['ffn1_norm']), p['ffn1_w_gate'].astype(BF16), p['ffn1_w_up'].astype(BF16),
              p['ffn1_w_down'].astype(BF16))
    z, h_mix = _in_proj(x1, row(p['mix_norm']), w_proj)
    y_conv = _conv(z, p['conv_w'], batch, seq)
    kvm = _mem_kv(mem2d, row(p['mem_norm']), p['w_mem_kv'].astype(BF16), row(p['mem_k_norm']))
    y_mem = _mem_attn(z, kvm, row(p['mem_q_norm']), batch, seq, mem_len)
    kvc = _compress(z, w1r, pe8, w2p, gkc, batch, seq)
    kaug, vsel, kwin, vwin = _kv_pack(z, twice(p['nsa_ks_norm']), twice(p['nsa_kw_norm']), batch, seq)
    y_nsa = _nsa(z, twice(p['nsa_q_norm']), kaug, vsel, kwin, vwin, kvc, _selection_weights(seq), batch, seq)
    merged = _merge(h_mix, y_conv, y_nsa, y_mem, w_gm, p['w_branch'].astype(BF16))
    x2 = _out_proj(x1, merged, p['w_o'].astype(BF16))
    return _ffn(x2, row(p['ffn2_norm']), p['ffn2_w_gate'].astype(BF16), p['ffn2_w_up'].astype(BF16),
                p['ffn2_w_down'].astype(BF16))


def kernel(x, mem, ffn1_norm, ffn1_w_gate, ffn1_w_up, ffn1_w_down, mix_norm, mem_norm, w_in, conv_w, nsa_q_norm, nsa_kc_norm, nsa_ks_norm, nsa_kw_norm, cmp_pe_k, cmp_w1_k, cmp_w2_k, cmp_pe_v, cmp_w1_v, cmp_w2_v, w_mem_kv, mem_q_norm, mem_k_norm, w_branch, w_o, ffn2_norm, ffn2_w_gate, ffn2_w_up, ffn2_w_down):
    batch, seq, d = x.shape
    mem_len = mem.shape[1]
    assert seq % (SEL_TK * 4) == 0 and seq // SLC_LEN <= LANES and seq >= WIN + Q_BLOCK
    assert SLC_LEN & (SLC_LEN - 1) == 0
    params = dict(
        ffn1_norm=ffn1_norm, ffn1_w_gate=ffn1_w_gate, ffn1_w_up=ffn1_w_up, ffn1_w_down=ffn1_w_down,
        mix_norm=mix_norm, mem_norm=mem_norm, w_in=w_in, conv_w=conv_w, nsa_q_norm=nsa_q_norm,
        nsa_kc_norm=nsa_kc_norm, nsa_ks_norm=nsa_ks_norm, nsa_kw_norm=nsa_kw_norm, cmp_pe_k=cmp_pe_k,
        cmp_w1_k=cmp_w1_k, cmp_w2_k=cmp_w2_k, cmp_pe_v=cmp_pe_v, cmp_w1_v=cmp_w1_v, cmp_w2_v=cmp_w2_v,
        w_mem_kv=w_mem_kv, mem_q_norm=mem_q_norm, mem_k_norm=mem_k_norm, w_branch=w_branch, w_o=w_o,
        ffn2_norm=ffn2_norm, ffn2_w_gate=ffn2_w_gate, ffn2_w_up=ffn2_w_up, ffn2_w_down=ffn2_w_down)
    x2d = x.reshape(batch * seq, d)
    mem2d = mem.reshape(batch * mem_len, d)
    for l in range(ffn1_norm.shape[0]):
        x2d = _layer(x2d, mem2d, batch, seq, mem_len, {k: v[l] for k, v in params.items()})
    return x2d.reshape(batch, seq, d)
```

```python
import functools
import math

import jax
import jax.numpy as jnp
import numpy as np
from jax import lax
from jax.experimental import pallas as pl
from jax.experimental.pallas import tpu as pltpu

F32 = jnp.float32
BF16 = jnp.bfloat16

EPS = 1e-6
NEG_INF = -1e30
FORCE_SCORE = 1e9

CONV_K = 3
NSA_HEAD_DIM = 64
NSA_GROUPS = 4
NSA_HPG = 4
NSA_HEADS = NSA_GROUPS * NSA_HPG
CMP_LEN = 32
CMP_STRIDE = 16
CMP_HIDDEN = 4 * NSA_HEAD_DIM
SLC_LEN = 64
SLC_TOP = 16
WIN = 512
Q_BLOCK = 128
MEM_HEADS = 4
N_BRANCH = 3

LANES = 128
SUBLANES = 8
VMEM_LIMIT_BYTES = 56 * 1024 * 1024

FFN_TM = 512
FFN_TF = 512
PROJ_TM = 1024
PROJ_TN = 768
CONV_TS = 1024
MEM_TM = 1024
PACK_TS = 512
SEL_TK = 512
NSA_QPB = 4
MERGE_TM = 1024
MERGE_TN = 512
OUT_TM = 512
OUT_TN = 2048

Z_CONV = 0
Z_QNSA = 3072
Z_QMEM = 4096
Z_KV = 5120
Z_GATE = 6656
Z_COLS = 6912


def _params(sem):
    return pltpu.CompilerParams(dimension_semantics=sem, vmem_limit_bytes=VMEM_LIMIT_BYTES)


def _rms(x, g):
    return x * lax.rsqrt(jnp.mean(x * x, axis=-1, keepdims=True) + EPS) * g


def _sigmoid(x):
    return 1.0 / (1.0 + jnp.exp(-x))


def _dot(a, b):
    return jnp.dot(a, b, preferred_element_type=F32)


def _dot_nt(a, b):
    return lax.dot_general(a, b, (((1,), (1,)), ((), ())), preferred_element_type=F32)


def _half_norm(x, g):
    lane = lax.broadcasted_iota(jnp.int32, x.shape, 1)
    lo = lane < NSA_HEAD_DIM
    xx = x * x
    s_lo = jnp.sum(jnp.where(lo, xx, 0.0), axis=-1, keepdims=True)
    s_hi = jnp.sum(jnp.where(lo, 0.0, xx), axis=-1, keepdims=True)
    r = jnp.where(lo, lax.rsqrt(s_lo / NSA_HEAD_DIM + EPS), lax.rsqrt(s_hi / NSA_HEAD_DIM + EPS))
    return x * r * g


def _split_halves(x):
    lane = lax.broadcasted_iota(jnp.int32, x.shape, 1)
    lo = lane < NSA_HEAD_DIM
    return jnp.where(lo, x, 0.0), jnp.where(lo, pltpu.roll(x, NSA_HEAD_DIM, axis=1), 0.0)


def _ffn_kernel(x_ref, g_ref, wg_ref, wu_ref, wd_ref, o_ref, h_ref):
    j = pl.program_id(1)

    @pl.when(j == 0)
    def _():
        h_ref[...] = _rms(x_ref[...], g_ref[...]).astype(BF16)
        o_ref[...] = jnp.zeros_like(o_ref)

    h = h_ref[...]
    gate = _dot(h, wg_ref[...])
    up = _dot(h, wu_ref[...])
    act = (gate * _sigmoid(gate) * up).astype(BF16)
    o_ref[...] += _dot(act, wd_ref[...])

    @pl.when(j == pl.num_programs(1) - 1)
    def _():
        o_ref[...] = x_ref[...] + 0.5 * o_ref[...]


def _ffn(x, g, wg, wu, wd):
    n, d = x.shape
    f = wg.shape[1]
    return pl.pallas_call(
        _ffn_kernel,
        out_shape=jax.ShapeDtypeStruct((n, d), F32),
        grid=(n // FFN_TM, f // FFN_TF),
        in_specs=[
            pl.BlockSpec((FFN_TM, d), lambda i, j: (i, 0)),
            pl.BlockSpec((1, d), lambda i, j: (0, 0)),
            pl.BlockSpec((d, FFN_TF), lambda i, j: (0, j)),
            pl.BlockSpec((d, FFN_TF), lambda i, j: (0, j)),
            pl.BlockSpec((FFN_TF, d), lambda i, j: (j, 0)),
        ],
        out_specs=pl.BlockSpec((FFN_TM, d), lambda i, j: (i, 0)),
        scratch_shapes=[pltpu.VMEM((FFN_TM, d), BF16)],
        compiler_params=_params(("parallel", "arbitrary")),
        name="ffn",
    )(x, g, wg, wu, wd)


def _proj_kernel(x_ref, g_ref, w_ref, o_ref, h_ref):
    @pl.when(pl.program_id(1) == 0)
    def _():
        h_ref[...] = _rms(x_ref[...], g_ref[...]).astype(BF16)

    o_ref[...] = _dot(h_ref[...], w_ref[...])


def _in_proj(x, g, w):
    n, d = x.shape
    c = w.shape[1]
    return pl.pallas_call(
        _proj_kernel,
        out_shape=(jax.ShapeDtypeStruct((n, c), F32), jax.ShapeDtypeStruct((n, d), BF16)),
        grid=(n // PROJ_TM, c // PROJ_TN),
        in_specs=[
            pl.BlockSpec((PROJ_TM, d), lambda i, j: (i, 0)),
            pl.BlockSpec((1, d), lambda i, j: (0, 0)),
            pl.BlockSpec((d, PROJ_TN), lambda i, j: (0, j)),
        ],
        out_specs=(pl.BlockSpec((PROJ_TM, PROJ_TN), lambda i, j: (i, j)),
                   pl.BlockSpec((PROJ_TM, d), lambda i, j: (i, 0))),
        compiler_params=_params(("parallel", "arbitrary")),
        name="in_proj",
    )(x, g, w)


def _conv_kernel(b_ref, c_ref, u_ref, cp_ref, up_ref, w_ref, o_ref, z_ref):
    ts = b_ref.shape[0]
    first = pl.program_id(1) == 0
    zp = cp_ref[...] * up_ref[...]
    z_ref[pl.ds(0, SUBLANES), :] = jnp.where(first, 0.0, zp)
    z = c_ref[...] * u_ref[...]
    z_ref[pl.ds(SUBLANES, ts), :] = z
    z1 = z_ref[pl.ds(SUBLANES - 1, ts), :]
    z2 = z_ref[pl.ds(SUBLANES - 2, ts), :]
    w = w_ref[...]
    y = b_ref[...] * (w[0:1, :] * z2 + w[1:2, :] * z1 + w[2:3, :] * z)
    o_ref[...] = y.astype(o_ref.dtype)


def _conv(z, conv_w, batch, seq):
    cw = conv_w.shape[1]
    nst = seq // CONV_TS
    rows_per = CONV_TS // SUBLANES
    cb = Z_CONV // cw

    def prev_map(col):
        return lambda b, i: (jnp.maximum((b * nst + i) * rows_per - 1, 0), col)

    return pl.pallas_call(
        _conv_kernel,
        out_shape=jax.ShapeDtypeStruct((batch * seq, cw), BF16),
        grid=(batch, nst),
        in_specs=[
            pl.BlockSpec((CONV_TS, cw), lambda b, i: (b * nst + i, cb)),
            pl.BlockSpec((CONV_TS, cw), lambda b, i: (b * nst + i, cb + 1)),
            pl.BlockSpec((CONV_TS, cw), lambda b, i: (b * nst + i, cb + 2)),
            pl.BlockSpec((SUBLANES, cw), prev_map(cb + 1)),
            pl.BlockSpec((SUBLANES, cw), prev_map(cb + 2)),
            pl.BlockSpec((CONV_K, cw), lambda b, i: (0, 0)),
        ],
        out_specs=pl.BlockSpec((CONV_TS, cw), lambda b, i: (b * nst + i, 0)),
        scratch_shapes=[pltpu.VMEM((CONV_TS + SUBLANES, cw), F32)],
        compiler_params=_params(("parallel", "arbitrary")),
        name="conv",
    )(z, z, z, z, z, conv_w)


def _mem_kv_kernel(m_ref, g_ref, w_ref, gk_ref, o_ref, *, n_key_tiles):
    h = _rms(m_ref[...], g_ref[...]).astype(BF16)
    kv = _dot(h, w_ref[...])
    is_key = pl.program_id(0) < n_key_tiles
    o_ref[...] = jnp.where(is_key, _rms(kv, gk_ref[...]), kv).astype(o_ref.dtype)


def _mem_kv(mem2d, g, w, gk):
    rows, d = mem2d.shape
    hd = gk.shape[1]
    cols = w.shape[1]
    return pl.pallas_call(
        functools.partial(_mem_kv_kernel, n_key_tiles=cols // (2 * hd)),
        out_shape=jax.ShapeDtypeStruct((rows, cols), BF16),
        grid=(cols // hd,),
        in_specs=[
            pl.BlockSpec((rows, d), lambda j: (0, 0)),
            pl.BlockSpec((1, d), lambda j: (0, 0)),
            pl.BlockSpec((d, hd), lambda j: (0, j)),
            pl.BlockSpec((1, hd), lambda j: (0, 0)),
        ],
        out_specs=pl.BlockSpec((rows, hd), lambda j: (0, j)),
        compiler_params=_params(("arbitrary",)),
        name="mem_kv",
    )(mem2d, g, w, gk)


def _mem_attn_kernel(q_ref, k_ref, v_ref, gq_ref, o_ref):
    hd = gq_ref.shape[1]
    scale = hd ** -0.5
    outs = []
    for h in range(MEM_HEADS):
        sl = slice(h * hd, (h + 1) * hd)
        q = (_rms(q_ref[:, sl], gq_ref[...]) * scale).astype(BF16)
        s = _dot_nt(q, k_ref[:, sl])
        e = jnp.exp(s - jnp.max(s, axis=-1, keepdims=True))
        p = e / jnp.sum(e, axis=-1, keepdims=True)
        outs.append(_dot(p.astype(BF16), v_ref[:, sl]))
    o_ref[...] = jnp.concatenate(outs, axis=1).astype(o_ref.dtype)


def _mem_attn(z, kvm, gq, batch, seq, mem_len):
    width = MEM_HEADS * gq.shape[1]
    nt = seq // MEM_TM
    return pl.pallas_call(
        _mem_attn_kernel,
        out_shape=jax.ShapeDtypeStruct((batch * seq, width), BF16),
        grid=(batch, nt),
        in_specs=[
            pl.BlockSpec((MEM_TM, width), lambda b, i: (b * nt + i, Z_QMEM // width)),
            pl.BlockSpec((mem_len, width), lambda b, i: (b, 0)),
            pl.BlockSpec((mem_len, width), lambda b, i: (b, 1)),
            pl.BlockSpec((1, gq.shape[1]), lambda b, i: (0, 0)),
        ],
        out_specs=pl.BlockSpec((MEM_TM, width), lambda b, i: (b * nt + i, 0)),
        compiler_params=_params(("parallel", "arbitrary")),
        name="mem_attn",
    )(z, kvm, kvm, gq)


def _compress_kernel(x_ref, w1_ref, pe_ref, w2_ref, g_ref, o_ref, b_ref):
    rows = o_ref.shape[1]
    is_key = pl.program_id(1) == 0
    ab = [jnp.zeros((rows, 2 * CMP_HIDDEN), F32) for _ in range(2)]
    peb = jnp.zeros((SUBLANES, 2 * CMP_HIDDEN), F32)
    for l in range(CMP_STRIDE):
        x = x_ref[pl.ds(l, rows, stride=CMP_STRIDE), :]
        w = w1_ref[l]
        halves = _split_halves(x)
        for gg in range(2):
            ab[gg] = ab[gg] + _dot(halves[gg][:, :NSA_HEAD_DIM].astype(BF16), w)
        peb = peb + _dot(pe_ref[l], w)
    bias = peb[0:1, :CMP_HIDDEN] + peb[1:2, CMP_HIDDEN:]
    b_ref[pl.ds(rows, SUBLANES), :] = jnp.zeros((SUBLANES, CMP_HIDDEN), F32)
    for gg in range(2):
        b_ref[pl.ds(0, rows), :] = ab[gg][:, CMP_HIDDEN:]
        hid = ab[gg][:, :CMP_HIDDEN] + b_ref[pl.ds(1, rows), :] + bias
        act = (hid * _sigmoid(hid)).astype(BF16)
        out = _dot(act, w2_ref[...])
        ms = jnp.sum(out * out, axis=-1, keepdims=True) / NSA_HEAD_DIM
        normed = out * lax.rsqrt(ms + EPS) * g_ref[...]
        o_ref[gg] = jnp.where(is_key, normed, out).astype(o_ref.dtype)


def _compress(z, w1r, pe8, w2p, gkc, batch, seq):
    rows = seq // CMP_STRIDE
    cb = Z_KV // LANES
    return pl.pallas_call(
        _compress_kernel,
        out_shape=jax.ShapeDtypeStruct((batch, 2, NSA_GROUPS, rows, LANES), BF16),
        grid=(batch, 2, NSA_GROUPS // 2),
        in_specs=[
            pl.BlockSpec((seq, LANES), lambda b, s, p: (b, cb + 2 * s + p)),
            pl.BlockSpec((None, CMP_STRIDE, NSA_HEAD_DIM, 2 * CMP_HIDDEN), lambda b, s, p: (s, 0, 0, 0)),
            pl.BlockSpec((None, CMP_STRIDE, SUBLANES, NSA_HEAD_DIM), lambda b, s, p: (s, 0, 0, 0)),
            pl.BlockSpec((None, CMP_HIDDEN, LANES), lambda b, s, p: (s, 0, 0)),
            pl.BlockSpec((1, LANES), lambda b, s, p: (0, 0)),
        ],
        out_specs=pl.BlockSpec((None, None, 2, rows, LANES), lambda b, s, p: (b, s, p, 0, 0)),
        scratch_shapes=[pltpu.VMEM((rows + SUBLANES, CMP_HIDDEN), F32)],
        compiler_params=_params(("parallel", "arbitrary", "arbitrary")),
        name="compress",
    )(z, w1r, pe8, w2p, gkc)


def _kv_pack_kernel(ks_ref, vs_ref, kw_ref, vw_ref, gks_ref, gkw_ref, ka_ref, vs_o, kw_o, vw_o):
    ts = ks_ref.shape[0]
    n_tiles = pl.num_programs(2) - 1
    i = pl.program_id(2)
    is_pad = i == n_tiles
    ksn = _split_halves(_half_norm(ks_ref[...], gks_ref[...]))
    kwn = _split_halves(_half_norm(kw_ref[...], gkw_ref[...]))
    vsh = _split_halves(vs_ref[...])
    vwh = _split_halves(vw_ref[...])
    pos = jnp.minimum(i, n_tiles - 1) * ts + lax.broadcasted_iota(jnp.int32, (ts, LANES), 0)
    blk = lax.broadcasted_iota(jnp.int32, (ts, LANES), 1)
    onehot = jnp.where(jnp.right_shift(pos, SLC_LEN.bit_length() - 1) == blk, 1.0, 0.0)
    ones_col = jnp.where(blk == NSA_HEAD_DIM, 1.0, 0.0)
    for gg in range(2):
        ka_ref[gg] = jnp.concatenate([ksn[gg], onehot], axis=1).astype(ka_ref.dtype)
        vs_o[gg] = (vsh[gg] + ones_col).astype(vs_o.dtype)
        kw_o[gg] = jnp.where(is_pad, ones_col, kwn[gg]).astype(kw_o.dtype)
        vw_o[gg] = jnp.where(is_pad, 0.0, vwh[gg] + ones_col).astype(vw_o.dtype)


def _kv_pack(z, gks, gkw, batch, seq):
    assert PACK_TS == WIN
    nst = seq // PACK_TS
    cb = Z_KV // LANES + 4

    def zspec(k):
        return pl.BlockSpec((PACK_TS, LANES), lambda b, p, i: (b * nst + jnp.minimum(i, nst - 1), cb + 2 * k + p))

    def ospec(w):
        return pl.BlockSpec((None, 2, PACK_TS, w), lambda b, p, i: (b, p, jnp.minimum(i, nst - 1), 0))

    def wspec(w):
        return pl.BlockSpec((None, 2, PACK_TS, w), lambda b, p, i: (b, p, (i + 1) % (nst + 1), 0))

    def oshape(rows, w):
        return jax.ShapeDtypeStruct((batch, NSA_GROUPS, rows, w), BF16)

    gspec = pl.BlockSpec((1, LANES), lambda b, p, i: (0, 0))
    return pl.pallas_call(
        _kv_pack_kernel,
        out_shape=(oshape(seq, 2 * LANES), oshape(seq, LANES), oshape(seq + WIN, LANES), oshape(seq + WIN, LANES)),
        grid=(batch, NSA_GROUPS // 2, nst + 1),
        in_specs=[zspec(0), zspec(1), zspec(2), zspec(3), gspec, gspec],
        out_specs=(ospec(2 * LANES), ospec(LANES), wspec(LANES), wspec(LANES)),
        compiler_params=_params(("parallel", "arbitrary", "arbitrary")),
        name="kv_pack",
    )(z, z, z, z, gks, gkw)


def _count_outranking(score_ref, cnt_ref, n_live):
    nb, nq = score_ref.shape
    n_chunks = nb // SUBLANES
    cnt_ref[...] = jnp.zeros(cnt_ref.shape, cnt_ref.dtype)
    for ci in range(n_chunks):
        @pl.when(ci < n_live)
        def _():
            sub = lax.broadcasted_iota(jnp.int32, (SUBLANES, nq), 0)
            chunks = [score_ref[SUBLANES * v:SUBLANES * (v + 1), :] for v in range(n_chunks)]
            cnt = [cnt_ref[SUBLANES * v:SUBLANES * (v + 1), :] for v in range(n_chunks)]
            for r in range(SUBLANES):
                row = jnp.broadcast_to(chunks[ci][r:r + 1, :], (SUBLANES, nq))
                for v, c in enumerate(chunks):
                    if v < ci:
                        beats = row > c
                    elif v > ci:
                        beats = row >= c
                    else:
                        beats = ((sub < r) & (row > c)) | ((sub > r) & (row >= c))
                    cnt[v] = cnt[v] + jnp.where(beats, 1, 0)
            for v in range(n_chunks):
                cnt_ref[SUBLANES * v:SUBLANES * (v + 1), :] = cnt[v]


def _nsa_kernel(q_ref, gate_ref, gq_ref, ka_ref, vs_ref, kw_ref, vw_ref, kc_ref, vc_ref, map_ref, band_ref,
                o_ref, m_ref, acc_ref, sa_ref, sb_ref, mta_ref, mtb_ref, qa_ref, score_ref, cnt_ref, og_ref):
    g = pl.program_id(1)
    t0 = pl.program_id(2) * (NSA_QPB * Q_BLOCK)
    sub = NSA_HPG * Q_BLOCK
    rows = NSA_QPB * sub
    n_q = NSA_QPB * Q_BLOCK
    n_cmp = kc_ref.shape[0]
    scale = NSA_HEAD_DIM ** -0.5 * math.log2(math.e)
    lane = lax.broadcasted_iota(jnp.int32, (1, LANES), 1)

    def block_rows(j):
        return slice(j * sub, (j + 1) * sub)

    def gate_col(gates, r, h):
        return jnp.sum(jnp.where(lane == r * NSA_HEADS + g * NSA_HPG + h, gates, 0.0), axis=-1, keepdims=True)

    def normalised(acc):
        denom = jnp.sum(jnp.where(lane == NSA_HEAD_DIM, acc, 0.0), axis=-1, keepdims=True)
        return jnp.where(lane < NSA_HEAD_DIM, acc / denom, 0.0)

    local = lax.broadcasted_iota(jnp.int32, (Q_BLOCK, 1), 0)
    t_blocks = [t0 + j * Q_BLOCK + local for j in range(NSA_QPB)]
    t = jnp.concatenate([tb for tb in t_blocks for _ in range(NSA_HPG)], axis=0)
    tq = jnp.concatenate(t_blocks, axis=0)
    cur = jnp.right_shift(tq, SLC_LEN.bit_length() - 1)
    blk = lax.broadcasted_iota(jnp.int32, (1, LANES), 1)

    def head(cols):
        heads = []
        for j in range(NSA_QPB):
            for pair in range(NSA_HPG // 2):
                qn = _half_norm(q_ref[j * Q_BLOCK:(j + 1) * Q_BLOCK, pair * LANES:(pair + 1) * LANES], gq_ref[...])
                heads.extend(_split_halves(qn * scale))
        q = jnp.where(lane == NSA_HEAD_DIM, NEG_INF, jnp.concatenate(heads, axis=0)).astype(BF16)
        qa_ref[:, 0:LANES] = q

        span = WIN + Q_BLOCK
        w_cols = span // LANES
        band = jnp.concatenate([band_ref[...]] * NSA_HPG, axis=0)

        def slab(j, c):
            k = j * w_cols + c
            per = (SEL_TK // LANES) * NSA_QPB
            ref = sa_ref if k < per else sb_ref
            k = k % per
            return ref.at[k // NSA_QPB, pl.ds((k % NSA_QPB) * sub, sub), :]

        for j in range(NSA_QPB):
            start = pl.multiple_of(t0 + j * Q_BLOCK, Q_BLOCK)
            s = _dot_nt(q[block_rows(j)], kw_ref[pl.ds(start, span), :]) + band
            mx = s[:, 0:LANES]
            for c in range(w_cols):
                slab(j, c)[...] = s[:, c * LANES:(c + 1) * LANES]
                if c:
                    mx = jnp.maximum(mx, s[:, c * LANES:(c + 1) * LANES])
            mta_ref[block_rows(j), :] = jnp.broadcast_to(jnp.max(mx, axis=-1, keepdims=True), mx.shape)
        o_win = []
        for j in range(NSA_QPB):
            start = pl.multiple_of(t0 + j * Q_BLOCK, Q_BLOCK)
            mt = mta_ref[block_rows(j), :]
            e = jnp.concatenate([jnp.exp2(slab(j, c)[...] - mt) for c in range(w_cols)], axis=1)
            o_win.append(_dot(e.astype(BF16), vw_ref[pl.ds(start, span), :]))

        s = _dot_nt(q, kc_ref[0:cols, :])
        cmp_end = lax.broadcasted_iota(jnp.int32, (1, cols), 1) * CMP_STRIDE + (CMP_LEN - 1)
        s = jnp.where(cmp_end <= t, s, NEG_INF)
        e = jnp.exp2(s - jnp.maximum(jnp.max(s, axis=-1, keepdims=True), 0.1 * NEG_INF))
        p = e / jnp.maximum(jnp.sum(e, axis=-1, keepdims=True), 1e-30)
        o_cmp = _dot(p.astype(BF16), vc_ref[0:cols, :])
        for j in range(NSA_QPB):
            gates = _sigmoid(gate_ref[j * Q_BLOCK:(j + 1) * Q_BLOCK, :])
            for h in range(NSA_HPG):
                hs = slice(h * Q_BLOCK, (h + 1) * Q_BLOCK)
                rs = slice(j * sub + h * Q_BLOCK, j * sub + (h + 1) * Q_BLOCK)
                og_ref[rs, :] = (gate_col(gates, 0, h) * o_cmp[rs]
                                 + gate_col(gates, 2, h) * normalised(o_win[j][hs]))
        imps = []
        for j in range(NSA_QPB):
            imp = p[j * sub:j * sub + Q_BLOCK]
            for h in range(1, NSA_HPG):
                imp = imp + p[j * sub + h * Q_BLOCK:j * sub + (h + 1) * Q_BLOCK]
            imps.append(imp)
        imp = jnp.concatenate(imps, axis=0)
        hi = imp.astype(BF16)
        r1 = imp - hi.astype(F32)
        mid = r1.astype(BF16)
        lo = (r1 - mid.astype(F32)).astype(BF16)
        wmap = map_ref[0:cols, :]
        imp_s = _dot(hi, wmap) + _dot(mid, wmap) + _dot(lo, wmap)
        forced = (blk == 0) | (blk == cur) | (blk == cur - 1)
        score = jnp.where(forced, FORCE_SCORE, jnp.where(blk <= cur, imp_s, NEG_INF))
        score_ref[...] = score.T

    n_col_blocks = n_cmp // LANES
    live_col_blocks = jnp.clip((t0 + n_q - CMP_LEN) // CMP_STRIDE // LANES + 1, 1, n_col_blocks)
    for n in range(1, n_col_blocks + 1):
        pl.when(live_col_blocks == n)(functools.partial(head, n * LANES))
    _count_outranking(score_ref, cnt_ref, (t0 + n_q - 1) // SLC_LEN // SUBLANES + 1)
    chosen = jnp.where(cnt_ref[...] < SLC_TOP, 0.0, NEG_INF).T
    bias = jnp.where(blk <= cur, chosen, NEG_INF).astype(BF16)
    bias_rows = [bias[j * Q_BLOCK:(j + 1) * Q_BLOCK] for j in range(NSA_QPB) for _ in range(NSA_HPG)]
    qa_ref[:, LANES:2 * LANES] = jnp.concatenate(bias_rows, axis=0)

    n_cols = SEL_TK // LANES
    m_ref[...] = jnp.full(m_ref.shape, NEG_INF, F32)
    acc_ref[...] = jnp.zeros(acc_ref.shape, F32)

    def tile_max(s_ref):
        mx = s_ref[0]
        for c in range(1, n_cols):
            mx = jnp.maximum(mx, s_ref[c])
        return jnp.broadcast_to(jnp.max(mx, axis=-1, keepdims=True), mx.shape)

    def scores(kt, s_ref, mt_ref):
        k0 = pl.multiple_of(kt * SEL_TK, SEL_TK)
        s = _dot_nt(qa_ref[...], ka_ref[pl.ds(k0, SEL_TK), :])
        for c in range(n_cols):
            s_ref[c] = s[:, c * LANES:(c + 1) * LANES]
        mt_ref[...] = tile_max(s_ref)

    def update(kt, s_ref, mt_ref):
        k0 = pl.multiple_of(kt * SEL_TK, SEL_TK)
        m_prev = m_ref[...]
        m_new = jnp.maximum(m_prev, mt_ref[...])
        alpha = jnp.exp2(m_prev - m_new)
        p = jnp.concatenate([jnp.exp2(s_ref[c] - m_new) for c in range(n_cols)], axis=1)
        acc_ref[...] = alpha * acc_ref[...] + _dot(p.astype(BF16), vs_ref[pl.ds(k0, SEL_TK), :])
        m_ref[...] = m_new

    n_full = t0 // SEL_TK
    odd = n_full % 2

    @pl.when(odd == 0)
    def _():
        scores(0, sa_ref, mta_ref)

    @pl.when(odd == 1)
    def _():
        scores(0, sb_ref, mtb_ref)
        scores(1, sa_ref, mta_ref)
        update(0, sb_ref, mtb_ref)

    def pair(j, carry):
        kt = odd + 2 * j
        scores(kt + 1, sb_ref, mtb_ref)
        update(kt, sa_ref, mta_ref)
        scores(kt + 2, sa_ref, mta_ref)
        update(kt + 1, sb_ref, mtb_ref)
        return carry

    lax.fori_loop(0, n_full // 2, pair, 0)
    for j in range(NSA_QPB):
        rs = block_rows(j)
        tj = t0 + j * Q_BLOCK
        dcol = (tj % SEL_TK) // LANES
        kpos = tj + lax.broadcasted_iota(jnp.int32, (1, LANES), 1)
        sa_ref[dcol, rs, :] = jnp.where(kpos <= t[rs], sa_ref[dcol, rs, :], NEG_INF)
    mta_ref[...] = tile_max(sa_ref)
    update(n_full, sa_ref, mta_ref)

    for j in range(NSA_QPB):
        qs = slice(j * Q_BLOCK, (j + 1) * Q_BLOCK)
        gates = _sigmoid(gate_ref[qs, :])
        outs = []
        for h in range(NSA_HPG):
            rs = slice(j * sub + h * Q_BLOCK, j * sub + (h + 1) * Q_BLOCK)
            outs.append(og_ref[rs, :] + gate_col(gates, 1, h) * normalised(acc_ref[rs, :]))
        pairs = [outs[2 * k] + pltpu.roll(outs[2 * k + 1], NSA_HEAD_DIM, axis=1) for k in range(NSA_HPG // 2)]
        o_ref[qs, :] = jnp.concatenate(pairs, axis=1).astype(o_ref.dtype)


def _nsa(z, gq, kaug, vsel, kwin, vwin, kvc, wmap, batch, seq):
    n_q = NSA_QPB * Q_BLOCK
    nq = seq // n_q
    gw = NSA_HPG * NSA_HEAD_DIM
    rows = NSA_HPG * n_q
    n_cmp = seq // CMP_STRIDE

    def kvspec(w, pad=0):
        return pl.BlockSpec((None, None, seq + pad, w), lambda b, g, i: (b, g, 0, 0))

    def cspec(s):
        return pl.BlockSpec((None, None, None, n_cmp, LANES), lambda b, g, i: (b, s, g, 0, 0))

    r_idx = np.arange(Q_BLOCK)[:, None]
    c_idx = np.arange(WIN + Q_BLOCK)[None, :]
    band = jnp.asarray(np.where((c_idx > r_idx) & (c_idx <= r_idx + WIN), 0.0, NEG_INF), F32)

    return pl.pallas_call(
        _nsa_kernel,
        out_shape=jax.ShapeDtypeStruct((batch * seq, NSA_HEADS * NSA_HEAD_DIM), BF16),
        grid=(batch, NSA_GROUPS, nq),
        in_specs=[
            pl.BlockSpec((n_q, gw), lambda b, g, i: (b * nq + i, Z_QNSA // gw + g)),
            pl.BlockSpec((n_q, LANES), lambda b, g, i: (b * nq + i, Z_GATE // LANES)),
            pl.BlockSpec((1, LANES), lambda b, g, i: (0, 0)),
            kvspec(2 * LANES), kvspec(LANES), kvspec(LANES, WIN), kvspec(LANES, WIN),
            cspec(0), cspec(1),
            pl.BlockSpec((n_cmp, LANES), lambda b, g, i: (0, 0)),
            pl.BlockSpec((Q_BLOCK, WIN + Q_BLOCK), lambda b, g, i: (0, 0)),
        ],
        out_specs=pl.BlockSpec((n_q, gw), lambda b, g, i: (b * nq + i, g)),
        scratch_shapes=[
            pltpu.VMEM((rows, LANES), F32), pltpu.VMEM((rows, LANES), F32),
            pltpu.VMEM((SEL_TK // LANES, rows, LANES), F32), pltpu.VMEM((SEL_TK // LANES, rows, LANES), F32),
            pltpu.VMEM((rows, LANES), F32), pltpu.VMEM((rows, LANES), F32),
            pltpu.VMEM((rows, 2 * LANES), BF16),
            pltpu.VMEM((LANES, n_q), F32), pltpu.VMEM((LANES, n_q), jnp.int32),
            pltpu.VMEM((rows, LANES), F32)],
        compiler_params=_params(("parallel", "parallel", "arbitrary")),
        name="nsa",
    )(z, z, gq, kaug, vsel, kwin, vwin, kvc, kvc, wmap, band)


def _selection_weights(seq):
    n_c = (seq - CMP_LEN) // CMP_STRIDE + 1
    n_s = seq // SLC_LEN
    ratio = SLC_LEN // CMP_STRIDE
    w = np.zeros((seq // CMP_STRIDE, LANES), np.float32)
    for j in range(n_s):
        for off in range(-(CMP_LEN // CMP_STRIDE - 1), ratio):
            ci = ratio * j + off
            if 0 <= ci < n_c:
                cs = ci * CMP_STRIDE
                ov = min(cs + CMP_LEN, (j + 1) * SLC_LEN) - max(cs, j * SLC_LEN)
                w[ci, j] = max(ov, 0) / CMP_LEN
    return jnp.asarray(w, BF16)


def _merge_kernel(h_ref, yc_ref, yn_ref, ym_ref, wgc_ref, wgn_ref, wgm_ref, wb_ref, o_ref):
    h = h_ref[...]
    merged = jnp.zeros(o_ref.shape, F32)
    for n, (y_ref, wg_ref) in enumerate(((yc_ref, wgc_ref), (yn_ref, wgn_ref), (ym_ref, wgm_ref))):
        merged = merged + _sigmoid(_dot(h, wg_ref[...])) * _dot(y_ref[...], wb_ref[n])
    o_ref[...] = merged.astype(o_ref.dtype)


def _merge(h, yc, yn, ym, wg, wb):
    n, d = h.shape
    bw = yc.shape[1]
    nc = d // MERGE_TN

    def gspec(branch):
        return pl.BlockSpec((d, MERGE_TN), lambda i, j: (0, branch * nc + j))

    return pl.pallas_call(
        _merge_kernel,
        out_shape=jax.ShapeDtypeStruct((n, d), BF16),
        grid=(n // MERGE_TM, nc),
        in_specs=[
            pl.BlockSpec((MERGE_TM, d), lambda i, j: (i, 0)),
            pl.BlockSpec((MERGE_TM, bw), lambda i, j: (i, 0)),
            pl.BlockSpec((MERGE_TM, bw), lambda i, j: (i, 0)),
            pl.BlockSpec((MERGE_TM, bw), lambda i, j: (i, 0)),
            gspec(0), gspec(1), gspec(2),
            pl.BlockSpec((N_BRANCH, bw, MERGE_TN), lambda i, j: (0, 0, j)),
        ],
        out_specs=pl.BlockSpec((MERGE_TM, MERGE_TN), lambda i, j: (i, j)),
        compiler_params=_params(("parallel", "arbitrary")),
        name="merge",
    )(h, yc, yn, ym, wg, wg, wg, wb)


def _out_proj_kernel(x_ref, m_ref, w_ref, o_ref):
    o_ref[...] = x_ref[...] + _dot(m_ref[...], w_ref[...])


def _out_proj(x, merged, wo):
    n, d = x.shape
    return pl.pallas_call(
        _out_proj_kernel,
        out_shape=jax.ShapeDtypeStruct((n, d), F32),
        grid=(n // OUT_TM, d // OUT_TN),
        in_specs=[
            pl.BlockSpec((OUT_TM, OUT_TN), lambda i, j: (i, j)),
            pl.BlockSpec((OUT_TM, d), lambda i, j: (i, 0)),
            pl.BlockSpec((d, OUT_TN), lambda i, j: (0, j)),
        ],
        out_specs=pl.BlockSpec((OUT_TM, OUT_TN), lambda i, j: (i, j)),
        compiler_params=_params(("parallel", "arbitrary")),
        name="out_proj",
    )(x, merged, wo)


def _layer(x2d, mem2d, batch, seq, mem_len, p):
    d = x2d.shape[1]
    bw = d // 2
    kvw = NSA_GROUPS * NSA_HEAD_DIM

    w_in = p['w_in'].astype(BF16)
    o_q = 3 * bw
    o_kv = o_q + NSA_HEADS * NSA_HEAD_DIM
    o_g = o_kv + 6 * kvw
    o_qm = o_g + 3 * NSA_HEADS
    o_gm = o_qm + bw
    gate_cols = w_in[:, o_g:o_qm].reshape(d, NSA_HEADS, 3).transpose(0, 2, 1).reshape(d, 3 * NSA_HEADS)
    w_proj = jnp.concatenate([
        w_in[:, :o_q], w_in[:, o_q:o_kv], w_in[:, o_qm:o_gm], w_in[:, o_kv:o_g], gate_cols,
        jnp.zeros((d, Z_COLS - Z_GATE - 3 * NSA_HEADS), BF16)], axis=1)
    w_gm = w_in[:, o_gm:]

    def cmp_weights(w1, w2, pe):
        w1l = w1.reshape(CMP_LEN, NSA_HEAD_DIM, CMP_HIDDEN)
        w1r = jnp.concatenate([w1l[:CMP_STRIDE], w1l[CMP_STRIDE:]], axis=2)
        pe8 = jnp.zeros((CMP_STRIDE, SUBLANES, NSA_HEAD_DIM), F32)
        pe8 = pe8.at[:, 0].set(pe[:CMP_STRIDE]).at[:, 1].set(pe[CMP_STRIDE:])
        w2p = jnp.pad(w2, ((0, 0), (0, LANES - NSA_HEAD_DIM)))
        return w1r.astype(BF16), pe8.astype(BF16), w2p.astype(BF16)

    ck = cmp_weights(p['cmp_w1_k'], p['cmp_w2_k'], p['cmp_pe_k'])
    cv = cmp_weights(p['cmp_w1_v'], p['cmp_w2_v'], p['cmp_pe_v'])
    w1r, pe8, w2p = (jnp.stack([a, b]) for a, b in zip(ck, cv))

    def row(v):
        return v.reshape(1, -1)

    def twice(v):
        return jnp.concatenate([v, v]).reshape(1, LANES)

    gkc = jnp.pad(p['nsa_kc_norm'], (0, LANES - NSA_HEAD_DIM)).reshape(1, LANES)

    x1 = _ffn(x2d, row(p['ffn1_norm']), p['ffn1_w_gate'].astype(BF16), p['ffn1_w_up'].astype(BF16),
              p['ffn1_w_down'].astype(BF16))
    z, h_mix = _in_proj(x1, row(p['mix_norm']), w_proj)
    y_conv = _conv(z, p['conv_w'], batch, seq)
    kvm = _mem_kv(mem2d, row(p['mem_norm']), p['w_mem_kv'].astype(BF16), row(p['mem_k_norm']))
    y_mem = _mem_attn(z, kvm, row(p['mem_q_norm']), batch, seq, mem_len)
    kvc = _compress(z, w1r, pe8, w2p, gkc, batch, seq)
    kaug, vsel, kwin, vwin = _kv_pack(z, twice(p['nsa_ks_norm']), twice(p['nsa_kw_norm']), batch, seq)
    y_nsa = _nsa(z, twice(p['nsa_q_norm']), kaug, vsel, kwin, vwin, kvc, _selection_weights(seq), batch, seq)
    merged = _merge(h_mix, y_conv, y_nsa, y_mem, w_gm, p['w_branch'].astype(BF16))
    x2 = _out_proj(x1, merged, p['w_o'].astype(BF16))
    return _ffn(x2, row(p['ffn2_norm']), p['ffn2_w_gate'].astype(BF16), p['ffn2_w_up'].astype(BF16),
                p['ffn2_w_down'].astype(BF16))


def kernel(x, mem, ffn1_norm, ffn1_w_gate, ffn1_w_up, ffn1_w_down, mix_norm, mem_norm, w_in, conv_w, nsa_q_norm, nsa_kc_norm, nsa_ks_norm, nsa_kw_norm, cmp_pe_k, cmp_w1_k, cmp_w2_k, cmp_pe_v, cmp_w1_v, cmp_w2_v, w_mem_kv, mem_q_norm, mem_k_norm, w_branch, w_o, ffn2_norm, ffn2_w_gate, ffn2_w_up, ffn2_w_down):
    batch, seq, d = x.shape
    mem_len = mem.shape[1]
    assert seq % (SEL_TK * 4) == 0 and seq // SLC_LEN <= LANES and seq >= WIN + Q_BLOCK
    assert SLC_LEN & (SLC_LEN - 1) == 0
    params = dict(
        ffn1_norm=ffn1_norm, ffn1_w_gate=ffn1_w_gate, ffn1_w_up=ffn1_w_up, ffn1_w_down=ffn1_w_down,
        mix_norm=mix_norm, mem_norm=mem_norm, w_in=w_in, conv_w=conv_w, nsa_q_norm=nsa_q_norm,
        nsa_kc_norm=nsa_kc_norm, nsa_ks_norm=nsa_ks_norm, nsa_kw_norm=nsa_kw_norm, cmp_pe_k=cmp_pe_k,
        cmp_w1_k=cmp_w1_k, cmp_w2_k=cmp_w2_k, cmp_pe_v=cmp_pe_v, cmp_w1_v=cmp_w1_v, cmp_w2_v=cmp_w2_v,
        w_mem_kv=w_mem_kv, mem_q_norm=mem_q_norm, mem_k_norm=mem_k_norm, w_branch=w_branch, w_o=w_o,
        ffn2_norm=ffn2_norm, ffn2_w_gate=ffn2_w_gate, ffn2_w_up=ffn2_w_up, ffn2_w_down=ffn2_w_down)
    x2d = x.reshape(batch * seq, d)
    mem2d = mem.reshape(batch * mem_len, d)
    for l in range(ffn1_norm.shape[0]):
        x2d = _layer(x2d, mem2d, batch, seq, mem_len, {k: v[l] for k, v in params.items()})
    return x2d.reshape(batch, seq, d)
```
